```python
import math
import jax, jax.numpy as jnp
from jax import lax
import numpy as np

D_MODEL = 2048
BATCH = 4
SEQ = 4096
DEPTH = 2

GRID_W = 64
CTX_LEN = 256
EPS = 1e-6
N_AB_LAYERS = (DEPTH + 1) // 2
N_C_LAYERS = DEPTH // 2

ATT_WIDTH = D_MODEL // 2
CONV_WIDTH = D_MODEL - ATT_WIDTH
DIFF_HEAD_DIM = 64
DIFF_HEADS = ATT_WIDTH // (2 * DIFF_HEAD_DIM)
DIFF_V_DIM = 2 * DIFF_HEAD_DIM
CONV_K = 31
Q_BLOCK = 128
ROPE_THETA = 10000.0
ROPE_AXIS_DIM = DIFF_HEAD_DIM // 2
AB_IN = 4 * ATT_WIDTH + 3 * CONV_WIDTH

HGRN_WIDTH = D_MODEL
HGRN_HEAD_DIM = 128
HGRN_HEADS = HGRN_WIDTH // HGRN_HEAD_DIM
HGRN_CHUNK = 64
C_IN = 5 * HGRN_WIDTH

kernel_name = 'hybrid_diffattn_conformer_hgrn2_dit'

F32 = jnp.float32


def rms_norm(x, g):
    xf = x.astype(F32)
    y = xf * lax.rsqrt(jnp.mean(xf * xf, axis=-1, keepdims=True) + EPS)
    return y.astype(x.dtype) * g


def layer_norm(x, g, b):
    xf = x.astype(F32)
    mu = jnp.mean(xf, axis=-1, keepdims=True)
    xc = xf - mu
    y = xc * lax.rsqrt(jnp.mean(xc * xc, axis=-1, keepdims=True) + EPS)
    return y.astype(x.dtype) * g + b


def axial_rope_tables(length):
    rows = length // GRID_W
    row = jnp.repeat(jnp.arange(rows), GRID_W).astype(F32)
    col = jnp.tile(jnp.arange(GRID_W), rows).astype(F32)
    inv = ROPE_THETA ** (-jnp.arange(0, ROPE_AXIS_DIM, 2, dtype=F32) / ROPE_AXIS_DIM)
    def axis_angles(pos):
        a = pos[:, None] * inv[None, :]
        return jnp.concatenate([a, a], axis=-1)
    ang = jnp.concatenate([axis_angles(row), axis_angles(col)], axis=-1)
    return jnp.cos(ang), jnp.sin(ang)


def apply_rope(x, cos, sin):
    xs = x.reshape(x.shape[:-1] + (2, 2, ROPE_AXIS_DIM // 2))
    rot = jnp.stack([-xs[..., 1, :], xs[..., 0, :]], axis=-2).reshape(x.shape)
    c = cos[None, :, None, None, :].astype(x.dtype)
    s = sin[None, :, None, None, :].astype(x.dtype)
    return x * c + rot * s


def diff_attend(q, k, v, lam):
    s = jnp.einsum('bqhmd,bkhmd->bhmqk', q, k, preferred_element_type=F32)
    p = jax.nn.softmax(s, axis=-1)
    a = p[:, :, 0] - lam * p[:, :, 1]
    return jnp.einsum('bhqk,bkhv->bqhv', a.astype(v.dtype), v)


def depthwise_conv(x, w, b):
    y = lax.conv_general_dilated(
        x, w[:, None, :].astype(x.dtype), (1,), [(CONV_K // 2, CONV_K // 2)],
        dimension_numbers=('NWC', 'WIO', 'NWC'), feature_group_count=x.shape[-1])
    return y + b


def chunk_scan(q, k, v, logf, s0):
    bsz, length, nh, _ = q.shape
    dv = v.shape[-1]
    n_chunks = length // HGRN_CHUNK
    def blk(t):
        return t.reshape(bsz, n_chunks, HGRN_CHUNK, nh, t.shape[-1]).transpose(1, 0, 3, 2, 4)
    q, k, v, logf = blk(q), blk(k), blk(v), blk(logf)
    b = jnp.cumsum(logf, axis=3)
    b_end = b[:, :, :, -1:, :]
    q_dec = q * jnp.exp(b)
    k_inv = k * jnp.exp(-b)
    k_end = k * jnp.exp(b_end - b)
    mask = jnp.tril(jnp.ones((HGRN_CHUNK, HGRN_CHUNK), dtype=bool))
    a = jnp.where(mask, jnp.einsum('nbhtd,nbhsd->nbhts', q_dec, k_inv), 0.0)
    o_intra = jnp.einsum('nbhts,nbhsv->nbhtv', a, v)
    decay_end = jnp.exp(b_end[:, :, :, 0, :])
    def step(s, xs):
        qd, ke, vv, de = xs
        o = jnp.einsum('bhtd,bhdv->bhtv', qd, s)
        s = de[..., None] * s + jnp.einsum('bhsd,bhsv->bhdv', ke, vv)
        return s, o
    s_fin, o_inter = lax.scan(step, s0, (q_dec, k_end, v, decay_end))
    o = (o_intra + o_inter).transpose(1, 0, 3, 2, 4).reshape(bsz, length, nh, dv)
    return o, s_fin


def attn_conv_layer(h_lat, h_ctx, cos, sin, w_in, w_out, qn_g, kn_g, lq1, lk1, lq2, lk2,
                    subln_g, conv_w, conv_b, cln_g, cln_b, layer_idx, need_ctx):
    bsz, seq, _ = h_lat.shape
    lam_init = 0.8 - 0.6 * math.exp(-0.3 * layer_idx)
    lam = (jnp.exp(jnp.sum(lq1.astype(F32) * lk1.astype(F32)))
           - jnp.exp(jnp.sum(lq2.astype(F32) * lk2.astype(F32))) + lam_init)
    cuts = [ATT_WIDTH, 2 * ATT_WIDTH, 3 * ATT_WIDTH, 4 * ATT_WIDTH,
            4 * ATT_WIDTH + CONV_WIDTH, 4 * ATT_WIDTH + 2 * CONV_WIDTH]
    q_scale = DIFF_HEAD_DIM ** -0.5

    def project(h):
        q, k, v, z_a, g_val, g_gate, z_b = jnp.split(h @ w_in, cuts, axis=-1)
        shp = h.shape[:2]
        q = rms_norm(q.reshape(shp + (DIFF_HEADS, 2, DIFF_HEAD_DIM)), qn_g)
        k = rms_norm(k.reshape(shp + (DIFF_HEADS, 2, DIFF_HEAD_DIM)), kn_g)
        v = v.reshape(shp + (DIFF_HEADS, DIFF_V_DIM))
        return q, k, v, z_a, g_val, g_gate, z_b

    def attn_out(o, z_a):
        o = rms_norm(o, subln_g) * (1.0 - lam_init)
        return o.reshape(o.shape[:2] + (ATT_WIDTH,)) * jax.nn.silu(z_a)

    def conv_out(g_val, g_gate, z_b):
        y = g_val * jax.nn.sigmoid(g_gate)
        y = depthwise_conv(y, conv_w, conv_b)
        y = layer_norm(y, cln_g, cln_b)
        return jax.nn.silu(y) * jax.nn.silu(z_b)

    qc, kc, vc, zac, gvc, ggc, zbc = project(h_ctx)
    ql, kl, vl, zal, gvl, ggl, zbl = project(h_lat)
    ql = apply_rope(ql, cos, sin) * q_scale
    kl = apply_rope(kl, cos, sin)
    k_all = jnp.concatenate([kc, kl], axis=1)
    v_all = jnp.concatenate([vc, vl], axis=1)
    nblk = seq // Q_BLOCK
    qb = ql.reshape((bsz, nblk, Q_BLOCK) + ql.shape[2:]).swapaxes(0, 1)
    ob = lax.map(lambda qq: diff_attend(qq, k_all, v_all, lam), qb)
    o_lat = ob.swapaxes(0, 1).reshape(bsz, seq, DIFF_HEADS, DIFF_V_DIM)
    y_lat = jnp.concatenate([attn_out(o_lat, zal), conv_out(gvl, ggl, zbl)], axis=-1) @ w_out
    y_ctx = None
    if need_ctx:
        o_ctx = diff_attend(qc * q_scale, kc, vc, lam)
        y_ctx = jnp.concatenate([attn_out(o_ctx, zac), conv_out(gvc, ggc, zbc)], axis=-1) @ w_out
    return y_lat, y_ctx


def hgrn_layer(h_lat, h_ctx, w_in, w_out, lb, onorm_g, need_ctx):
    bsz = h_lat.shape[0]

    def heads(t):
        return t.reshape(t.shape[:2] + (HGRN_HEADS, HGRN_HEAD_DIM))

    def project(h):
        q, i, u_fw, u_bw, z = jnp.split(h @ w_in, 5, axis=-1)
        q = heads(jax.nn.silu(q).astype(F32))
        v = heads(i.astype(F32))
        dirs = []
        for u, lbd in ((u_fw, lb[0]), (u_bw, lb[1])):
            uf = u.astype(F32)
            logf = jnp.log(lbd + (1.0 - lbd) * jax.nn.sigmoid(uf))
            k = (1.0 - lbd) * jax.nn.sigmoid(-uf)
            dirs.append((heads(logf), heads(k)))
        return q, v, dirs, z

    def flip(t):
        return jnp.flip(t, axis=1)

    def readout(o, z):
        o = rms_norm(o, onorm_g).astype(z.dtype)
        return (o.reshape(o.shape[:2] + (HGRN_WIDTH,)) * jax.nn.silu(z)) @ w_out

    qc, vc, (fwc, bwc), zc = project(h_ctx)
    ql, vl, (fwl, bwl), zl = project(h_lat)
    zero = jnp.zeros((bsz, HGRN_HEADS, HGRN_HEAD_DIM, HGRN_HEAD_DIM), F32)
    o_cf, s_cf = chunk_scan(qc, fwc[1], vc, fwc[0], zero)
    o_cb, s_cb = chunk_scan(flip(qc), flip(bwc[1]), flip(vc), flip(bwc[0]), zero)
    o_lf, _ = chunk_scan(ql, fwl[1], vl, fwl[0], s_cf)
    o_lb, _ = chunk_scan(flip(ql), flip(bwl[1]), flip(vl), flip(bwl[0]), s_cb)
    y_lat = readout(o_lf + flip(o_lb), zl)
    y_ctx = readout(o_cf + flip(o_cb), zc) if need_ctx else None
    return y_lat, y_ctx


def setup_inputs(seed: int = 0) -> dict:
    key = jax.random.key(seed)
    ks = jax.random.split(key, 24)
    D = D_MODEL
    def nrm(k, shape, scale):
        return jax.random.normal(k, shape, F32) * scale
    return {
        'x': nrm(ks[0], (BATCH, SEQ, D), 1.0),
        'c': nrm(ks[1], (BATCH, D), 1.0),
        'ctx': nrm(ks[2], (BATCH, CTX_LEN, D), 1.0),
        'c_ctx': nrm(ks[3], (D,), 1.0),
        'w_ada': nrm(ks[4], (DEPTH, D, 3 * D), 0.5 * D ** -0.5),
        'b_ada': nrm(ks[5], (DEPTH, 3 * D), 0.02),
        'norm_g': 1.0 + nrm(ks[6], (DEPTH, D), 0.02),
        'w_in_ab': nrm(ks[7], (N_AB_LAYERS, D, AB_IN), D ** -0.5),
        'w_out_ab': nrm(ks[8], (N_AB_LAYERS, D, D), D ** -0.5),
        'qn_g': 1.0 + nrm(ks[9], (N_AB_LAYERS, DIFF_HEAD_DIM), 0.02),
        'kn_g': 1.0 + nrm(ks[10], (N_AB_LAYERS, DIFF_HEAD_DIM), 0.02),
        'lam_q1': nrm(ks[11], (N_AB_LAYERS, DIFF_HEAD_DIM), 0.1),
        'lam_k1': nrm(ks[12], (N_AB_LAYERS, DIFF_HEAD_DIM), 0.1),
        'lam_q2': nrm(ks[13], (N_AB_LAYERS, DIFF_HEAD_DIM), 0.1),
        'lam_k2': nrm(ks[14], (N_AB_LAYERS, DIFF_HEAD_DIM), 0.1),
        'subln_g': 1.0 + nrm(ks[15], (N_AB_LAYERS, DIFF_V_DIM), 0.02),
        'conv_w': nrm(ks[16], (N_AB_LAYERS, CONV_K, CONV_WIDTH), CONV_K ** -0.5),
        'conv_b': nrm(ks[17], (N_AB_LAYERS, CONV_WIDTH), 0.02),
        'cln_g': 1.0 + nrm(ks[18], (N_AB_LAYERS, CONV_WIDTH), 0.02),
        'cln_b': nrm(ks[19], (N_AB_LAYERS, CONV_WIDTH), 0.02),
        'w_in_c': nrm(ks[20], (N_C_LAYERS, D, C_IN), D ** -0.5),
        'w_out_c': nrm(ks[21], (N_C_LAYERS, D, D), D ** -0.5),
        'lb_gamma': nrm(ks[22], (2, DEPTH, HGRN_WIDTH), 0.5),
        'onorm_g': 1.0 + nrm(ks[23], (N_C_LAYERS, HGRN_HEAD_DIM), 0.02),
    }


def reference(x, c, ctx, c_ctx, w_ada, b_ada, norm_g, w_in_ab, w_out_ab, qn_g, kn_g,
              lam_q1, lam_k1, lam_q2, lam_k2, subln_g, conv_w, conv_b, cln_g, cln_b,
              w_in_c, w_out_c, lb_gamma, onorm_g):
    seq = x.shape[1]
    cos, sin = axial_rope_tables(seq)
    p = jax.nn.softmax(lb_gamma.astype(F32), axis=1)
    lb_all = jnp.cumsum(p, axis=1) - p[:, :1]
    sc = jax.nn.silu(c)
    scc = jax.nn.silu(c_ctx)
    h_ctx_stream = ctx
    for l in range(DEPTH):
        need_ctx = l < DEPTH - 1
        shift, scale, gate = jnp.split(sc @ w_ada[l] + b_ada[l], 3, axis=-1)
        shift_c, scale_c, gate_c = jnp.split(scc @ w_ada[l] + b_ada[l], 3, axis=-1)
        h_lat = rms_norm(x, norm_g[l]) * (1.0 + scale[:, None, :]) + shift[:, None, :]
        h_ctx = rms_norm(h_ctx_stream, norm_g[l]) * (1.0 + scale_c) + shift_c
        j = l // 2
        if l % 2 == 0:
            y_lat, y_ctx = attn_conv_layer(
                h_lat, h_ctx, cos, sin, w_in_ab[j], w_out_ab[j], qn_g[j], kn_g[j],
                lam_q1[j], lam_k1[j], lam_q2[j], lam_k2[j], subln_g[j],
                conv_w[j], conv_b[j], cln_g[j], cln_b[j], l, need_ctx)
        else:
            y_lat, y_ctx = hgrn_layer(h_lat, h_ctx, w_in_c[j], w_out_c[j], lb_all[:, l],
                                      onorm_g[j], need_ctx)
        x = x + gate[:, None, :] * y_lat.astype(x.dtype)
        if need_ctx:
            h_ctx_stream = h_ctx_stream + gate_c * y_ctx.astype(h_ctx_stream.dtype)
    return x
```

```python
import functools
import math

import jax
import jax.numpy as jnp
from jax import lax
from jax.experimental import pallas as pl
from jax.experimental.pallas import tpu as pltpu

F32 = jnp.float32
BF16 = jnp.bfloat16

EPS = 1e-6
GRID_W = 64
ROPE_THETA = 10000.0
HEAD_DIM = 64
HEAD_LANES = 128
N_HEADS = 8
CONV_K = 31
CONV_HALO = 16
HGRN_HEADS = 16
HGRN_CHUNK = 64

KEY_CHUNK = 256
VMEM_LIMIT = 48 * 1024 * 1024


def _silu(x):
    return x * jax.nn.sigmoid(x)


def _params(*sem):
    return pltpu.CompilerParams(dimension_semantics=sem, vmem_limit_bytes=VMEM_LIMIT)


def _ada_kernel(c_ref, w_ref, b_ref, o_ref):
    sc = _silu(c_ref[...])
    o_ref[0] = jnp.dot(sc, w_ref[0], precision=lax.Precision.HIGHEST,
                       preferred_element_type=F32) + b_ref[0]


def _ada(cond_rows, w_ada, b_ada):
    depth, d, n = w_ada.shape
    tn = 768
    return pl.pallas_call(
        _ada_kernel,
        grid=(depth, n // tn),
        in_specs=[pl.BlockSpec((8, d), lambda l, j: (0, 0)),
                  pl.BlockSpec((1, d, tn), lambda l, j: (l, 0, j)),
                  pl.BlockSpec((1, 1, tn), lambda l, j: (l, 0, j))],
        out_specs=pl.BlockSpec((1, 8, tn), lambda l, j: (l, 0, j)),
        out_shape=jax.ShapeDtypeStruct((depth, 8, n), F32),
        compiler_params=_params("arbitrary", "arbitrary"),
        name="ada",
    )(cond_rows, w_ada, b_ada.reshape(depth, 1, n))


def _group_mean_sq(x, gmat):
    sq = x * x
    hi = sq.astype(BF16)
    lo = (sq - hi.astype(F32)).astype(BF16)
    return (jnp.dot(hi, gmat, preferred_element_type=F32)
            + jnp.dot(lo, gmat, preferred_element_type=F32))


def _proj_kernel(*refs, n_qk_tiles, rope, q_scale, row_chunk):
    if n_qk_tiles:
        (x_ref, g_ref, sc_ref, sh_ref, w_ref, qg_ref, kg_ref, gmat_ref,
         cos_ref, sa_ref, sb_ref, o_ref, xn_ref) = refs
    else:
        x_ref, g_ref, sc_ref, sh_ref, w_ref, o_ref, xn_ref = refs
    j = pl.program_id(1)
    tm, tn = o_ref.shape

    @pl.when(j == 0)
    def _normalise():
        gain = g_ref[...]
        mod = 1.0 + sc_ref[0]
        shift = sh_ref[0]

        def body(r, carry):
            r0 = pl.multiple_of(r * row_chunk, row_chunk)
            x = x_ref[pl.ds(r0, row_chunk), :]
            ms = jnp.mean(x * x, axis=-1, keepdims=True)
            y = x * lax.rsqrt(ms + EPS) * gain
            xn_ref[pl.ds(r0, row_chunk), :] = (y * mod + shift).astype(BF16)
            return carry

        lax.fori_loop(0, tm // row_chunk, body, 0)

    acc = jnp.dot(xn_ref[...], w_ref[...], preferred_element_type=F32)

    if not n_qk_tiles:
        o_ref[...] = acc
        return

    def norm_rope(gain_ref, scale):
        gmat = gmat_ref[...]
        gain = gain_ref[...]
        for c in range(tn // HEAD_LANES):
            xs = acc[:, c * HEAD_LANES:(c + 1) * HEAD_LANES]
            y = xs * lax.rsqrt(_group_mean_sq(xs, gmat) + EPS) * gain
            if rope:
                y = (y * cos_ref[...]
                     + pltpu.roll(y, HEAD_LANES - 16, axis=1) * sa_ref[...]
                     + pltpu.roll(y, 16, axis=1) * sb_ref[...])
            if scale != 1.0:
                y = y * scale
            o_ref[:, c * HEAD_LANES:(c + 1) * HEAD_LANES] = y

    @pl.when(j < n_qk_tiles)
    def _q():
        norm_rope(qg_ref, q_scale)

    @pl.when((j >= n_qk_tiles) & (j < 2 * n_qk_tiles))
    def _k():
        norm_rope(kg_ref, 1.0)

    @pl.when(j >= 2 * n_qk_tiles)
    def _rest():
        o_ref[...] = acc


def _proj(x2d, gain, scale, shift, w_bf, rows_per_group, *, tm, tn, qk=None, name):
    m, d = x2d.shape
    n = w_bf.shape[1]
    groups_per_tile = rows_per_group // tm
    in_specs = [
        pl.BlockSpec((tm, d), lambda i, j: (i, 0)),
        pl.BlockSpec((1, d), lambda i, j: (0, 0)),
        pl.BlockSpec((1, 1, d), lambda i, j: (i // groups_per_tile, 0, 0)),
        pl.BlockSpec((1, 1, d), lambda i, j: (i // groups_per_tile, 0, 0)),
        pl.BlockSpec((d, tn), lambda i, j: (0, j)),
    ]
    args = [x2d, gain.reshape(1, d), scale, shift, w_bf]
    n_qk_tiles, rope, q_scale = 0, False, 1.0
    if qk is not None:
        qg, kg, gmat, cos, sa, sb, q_width, q_scale = qk
        n_qk_tiles = q_width // tn
        rope = cos is not None
        const = lambda i, j: (0, 0)
        in_specs += [pl.BlockSpec((1, HEAD_LANES), const), pl.BlockSpec((1, HEAD_LANES), const),
                     pl.BlockSpec((HEAD_LANES, HEAD_LANES), const)]
        args += [qg, kg, gmat]
        if rope:
            pos_tiles = cos.shape[0] // tm
            pos = lambda i, j: (i % pos_tiles, 0)
            in_specs += [pl.BlockSpec((tm, HEAD_LANES), pos)] * 3
            args += [cos, sa, sb]
        else:
            dummy = jnp.zeros((8, HEAD_LANES), F32)
            in_specs += [pl.BlockSpec((8, HEAD_LANES), const)] * 3
            args += [dummy, dummy, dummy]
    kern = functools.partial(_proj_kernel, n_qk_tiles=n_qk_tiles, rope=rope, q_scale=q_scale,
                             row_chunk=64)
    return pl.pallas_call(
        kern,
        grid=(m // tm, n // tn),
        in_specs=in_specs,
        out_specs=pl.BlockSpec((tm, tn), lambda i, j: (i, j)),
        out_shape=jax.ShapeDtypeStruct((m, n), F32),
        scratch_shapes=[pltpu.VMEM((tm, d), BF16)],
        compiler_params=_params("arbitrary", "arbitrary"),
        name=name,
    )(*args)


def _attn_kernel(*refs, n_ctx_chunks, n_lat_chunks, lam_init):
    if n_lat_chunks:
        (lam_ref, q_ref, kc_ref, vc_ref, kl_ref, vl_ref, za_ref, sg_ref,
         o_ref, kb_ref, vt_ref) = refs
    else:
        lam_ref, q_ref, kc_ref, vc_ref, za_ref, sg_ref, o_ref, kb_ref, vt_ref = refs
    qi = pl.program_id(2)
    tq = q_ref.shape[1]
    n_chunks = n_ctx_chunks + n_lat_chunks

    @pl.when(qi == 0)
    def _stage_keys():
        for c in range(n_ctx_chunks):
            rows = slice(c * KEY_CHUNK, (c + 1) * KEY_CHUNK)
            kb_ref[c] = kc_ref[0, rows, :].astype(BF16)
            vt_ref[c] = vc_ref[0, rows, :].T.astype(BF16)
        if n_lat_chunks:
            def body(c, carry):
                r0 = pl.multiple_of(c * KEY_CHUNK, KEY_CHUNK)
                kb_ref[n_ctx_chunks + c] = kl_ref[0, pl.ds(r0, KEY_CHUNK), :].astype(BF16)
                vt_ref[n_ctx_chunks + c] = vl_ref[0, pl.ds(r0, KEY_CHUNK), :].T.astype(BF16)
                return carry
            lax.fori_loop(0, n_lat_chunks, body, 0)

    q = q_ref[0]
    lane = lax.broadcasted_iota(jnp.int32, q.shape, 1)
    q_maps = (jnp.where(lane < HEAD_DIM, q, 0.0).astype(BF16),
              jnp.where(lane >= HEAD_DIM, q, 0.0).astype(BF16))

    def body(c, carry):
        kblk = kb_ref[c]
        vtb = vt_ref[c]
        out = []
        for mp in range(2):
            m_old, l_old, o_old = carry[3 * mp:3 * mp + 3]
            s = lax.dot_general(kblk, q_maps[mp], (((1,), (1,)), ((), ())),
                                preferred_element_type=F32)
            m_new = jnp.maximum(m_old, jnp.max(s, axis=0, keepdims=True))
            p = jnp.exp(s - m_new)
            alpha = jnp.exp(m_old - m_new)
            l_new = alpha * l_old + jnp.sum(p, axis=0, keepdims=True)
            o_new = alpha * o_old + jnp.dot(vtb, p.astype(BF16), preferred_element_type=F32)
            out += [m_new, l_new, o_new]
        return tuple(out)

    m0 = jnp.full((1, tq), -1e30, F32)
    l0 = jnp.zeros((1, tq), F32)
    o0 = jnp.zeros((HEAD_LANES, tq), F32)
    _, l_a, o_a, _, l_b, o_b = lax.fori_loop(0, n_chunks, body, (m0, l0, o0, m0, l0, o0))

    o = (o_a / l_a - lam_ref[0] * (o_b / l_b)).T
    ms = jnp.mean(o * o, axis=-1, keepdims=True)
    y = o * lax.rsqrt(ms + EPS) * sg_ref[...] * (1.0 - lam_init)
    o_ref[0] = (y * _silu(za_ref[0])).astype(BF16)


def _attention(lam, proj_q, proj_ctx, proj_lat, subln_g, *, tq, att_width, lam_init, name):
    bsz, lq, _ = proj_q.shape
    n_ctx = proj_ctx.shape[1]
    nh = att_width // HEAD_LANES
    n_ctx_chunks = n_ctx // KEY_CHUNK
    n_lat_chunks = 0 if proj_lat is None else proj_lat.shape[1] // KEY_CHUNK
    k_off, v_off, z_off = nh, 2 * nh, 3 * nh
    in_specs = [
        pl.BlockSpec(memory_space=pltpu.SMEM),
        pl.BlockSpec((1, tq, HEAD_LANES), lambda b, h, i: (b, i, h)),
        pl.BlockSpec((1, n_ctx, HEAD_LANES), lambda b, h, i: (b, 0, k_off + h)),
        pl.BlockSpec((1, n_ctx, HEAD_LANES), lambda b, h, i: (b, 0, v_off + h)),
    ]
    args = [lam, proj_q, proj_ctx, proj_ctx]
    if n_lat_chunks:
        n_lat = proj_lat.shape[1]
        in_specs += [pl.BlockSpec((1, n_lat, HEAD_LANES), lambda b, h, i: (b, 0, k_off + h)),
                     pl.BlockSpec((1, n_lat, HEAD_LANES), lambda b, h, i: (b, 0, v_off + h))]
        args += [proj_lat, proj_lat]
    in_specs += [pl.BlockSpec((1, tq, HEAD_LANES), lambda b, h, i: (b, i, z_off + h)),
                 pl.BlockSpec((1, HEAD_LANES), lambda b, h, i: (0, 0))]
    args += [proj_q, subln_g.reshape(1, HEAD_LANES)]
    n_chunks = n_ctx_chunks + n_lat_chunks
    kern = functools.partial(_attn_kernel, n_ctx_chunks=n_ctx_chunks, n_lat_chunks=n_lat_chunks,
                             lam_init=lam_init)
    return pl.pallas_call(
        kern,
        grid=(bsz, nh, lq // tq),
        in_specs=in_specs,
        out_specs=pl.BlockSpec((1, tq, HEAD_LANES), lambda b, h, i: (b, i, h)),
        out_shape=jax.ShapeDtypeStruct((bsz, lq, att_width), BF16),
        scratch_shapes=[pltpu.VMEM((n_chunks, KEY_CHUNK, HEAD_LANES), BF16),
                        pltpu.VMEM((n_chunks, HEAD_LANES, KEY_CHUNK), BF16)],
        compiler_params=_params("arbitrary", "arbitrary", "arbitrary"),
        name=name,
    )(*args)


def _conv_kernel(gv_ref, gg_ref, zb_ref, gvp_ref, ggp_ref, gvn_ref, ggn_ref,
                 w_ref, b_ref, lg_ref, lb_ref, o_ref, ypad_ref, *, row_chunk):
    i = pl.program_id(1)
    n_tiles = pl.num_programs(1)
    tl = o_ref.shape[1]

    def glu(val, gate):
        return val * jax.nn.sigmoid(gate)

    prev = glu(gvp_ref[0], ggp_ref[0])
    ypad_ref[0:CONV_HALO, :] = jnp.where(i > 0, prev, 0.0)
    nxt = glu(gvn_ref[0], ggn_ref[0])
    ypad_ref[CONV_HALO + tl:2 * CONV_HALO + tl, :] = jnp.where(i < n_tiles - 1, nxt, 0.0)
    for r in range(tl // row_chunk):
        rows = slice(r * row_chunk, (r + 1) * row_chunk)
        ypad_ref[CONV_HALO + r * row_chunk:CONV_HALO + (r + 1) * row_chunk, :] = glu(
            gv_ref[0, rows, :], gg_ref[0, rows, :])

    first = CONV_HALO - CONV_K // 2
    for r in range(tl // row_chunk):
        base = r * row_chunk + first
        acc = w_ref[0:1, :] * ypad_ref[base:base + row_chunk, :] + b_ref[...]
        for t in range(1, CONV_K):
            acc = acc + w_ref[t:t + 1, :] * ypad_ref[base + t:base + t + row_chunk, :]
        mu = jnp.mean(acc, axis=-1, keepdims=True)
        xc = acc - mu
        var = jnp.mean(xc * xc, axis=-1, keepdims=True)
        y = xc * lax.rsqrt(var + EPS) * lg_ref[...] + lb_ref[...]
        rows = slice(r * row_chunk, (r + 1) * row_chunk)
        o_ref[0, rows, :] = (_silu(y) * _silu(zb_ref[0, rows, :])).astype(BF16)


def _conv_branch(proj, conv_w, conv_b, cln_g, cln_b, *, tl, col0, width, name):
    bsz, length, _ = proj.shape
    n_tiles = length // tl
    halo_per_tile = tl // CONV_HALO
    last_halo = length // CONV_HALO - 1
    main = lambda off: pl.BlockSpec((1, tl, width), lambda b, i: (b, i, col0 + off))
    prev = lambda off: pl.BlockSpec(
        (1, CONV_HALO, width), lambda b, i: (b, jnp.maximum(i * halo_per_tile - 1, 0), col0 + off))
    nxt = lambda off: pl.BlockSpec(
        (1, CONV_HALO, width),
        lambda b, i: (b, jnp.minimum((i + 1) * halo_per_tile, last_halo), col0 + off))
    row = lambda k: pl.BlockSpec((k, width), lambda b, i: (0, 0))
    return pl.pallas_call(
        functools.partial(_conv_kernel, row_chunk=32),
        grid=(bsz, n_tiles),
        in_specs=[main(0), main(1), main(2), prev(0), prev(1), nxt(0), nxt(1),
                  row(CONV_K), row(1), row(1), row(1)],
        out_specs=pl.BlockSpec((1, tl, width), lambda b, i: (b, i, 0)),
        out_shape=jax.ShapeDtypeStruct((bsz, length, width), BF16),
        scratch_shapes=[pltpu.VMEM((tl + 2 * CONV_HALO, width), F32)],
        compiler_params=_params("arbitrary", "arbitrary"),
        name=name,
    )(proj, proj, proj, proj, proj, proj, proj,
      conv_w, conv_b.reshape(1, width), cln_g.reshape(1, width), cln_b.reshape(1, width))


def _out_ab_kernel(a_ref, c_ref, wa_ref, wc_ref, x_ref, gate_ref, o_ref):
    y = (jnp.dot(a_ref[...], wa_ref[...], preferred_element_type=F32)
         + jnp.dot(c_ref[...], wc_ref[...], preferred_element_type=F32))
    o_ref[...] = x_ref[...] + gate_ref[0] * y


def _out_ab(att, conv, w_bf, x2d, gate, rows_per_group, *, tm, tn, name):
    m, d = x2d.shape
    ka = att.shape[1]
    kc = conv.shape[1]
    groups_per_tile = rows_per_group // tm
    return pl.pallas_call(
        _out_ab_kernel,
        grid=(m // tm, d // tn),
        in_specs=[pl.BlockSpec((tm, ka), lambda i, j: (i, 0)),
                  pl.BlockSpec((tm, kc), lambda i, j: (i, 0)),
                  pl.BlockSpec((ka, tn), lambda i, j: (0, j)),
                  pl.BlockSpec((kc, tn), lambda i, j: (ka // kc, j)),
                  pl.BlockSpec((tm, tn), lambda i, j: (i, j)),
                  pl.BlockSpec((1, 1, tn), lambda i, j: (i // groups_per_tile, 0, j))],
        out_specs=pl.BlockSpec((tm, tn), lambda i, j: (i, j)),
        out_shape=jax.ShapeDtypeStruct((m, d), F32),
        compiler_params=_params("arbitrary", "arbitrary"),
        name=name,
    )(att, conv, w_bf, w_bf, x2d, gate)


def _out_c_kernel(of_ref, ob_ref, z_ref, g_ref, w_ref, x_ref, gate_ref, o_ref, y_ref, *, row_chunk):
    j = pl.program_id(1)
    tm, d = y_ref.shape

    @pl.when(j == 0)
    def _readout():
        def body(r, carry):
            r0 = pl.multiple_of(r * row_chunk, row_chunk)
            rows = pl.ds(r0, row_chunk)
            for h in range(d // HEAD_LANES):
                cols = slice(h * HEAD_LANES, (h + 1) * HEAD_LANES)
                o = of_ref[rows, cols] + ob_ref[rows, cols]
                ms = jnp.mean(o * o, axis=-1, keepdims=True)
                y = o * lax.rsqrt(ms + EPS) * g_ref[...]
                y_ref[rows, cols] = (y * _silu(z_ref[rows, cols])).astype(BF16)
            return carry
        lax.fori_loop(0, tm // row_chunk, body, 0)

    y = jnp.dot(y_ref[...], w_ref[...], preferred_element_type=F32)
    o_ref[...] = x_ref[...] + gate_ref[0] * y


def _out_c(o_f, o_b, proj, z_col, onorm_g, w_bf, x2d, gate, rows_per_group, *, tm, tn, name):
    m, d = x2d.shape
    groups_per_tile = rows_per_group // tm
    return pl.pallas_call(
        functools.partial(_out_c_kernel, row_chunk=64),
        grid=(m // tm, d // tn),
        in_specs=[pl.BlockSpec((tm, d), lambda i, j: (i, 0)),
                  pl.BlockSpec((tm, d), lambda i, j: (i, 0)),
                  pl.BlockSpec((tm, d), lambda i, j: (i, z_col)),
                  pl.BlockSpec((1, HEAD_LANES), lambda i, j: (0, 0)),
                  pl.BlockSpec((d, tn), lambda i, j: (0, j)),
                  pl.BlockSpec((tm, tn), lambda i, j: (i, j)),
                  pl.BlockSpec((1, 1, tn), lambda i, j: (i // groups_per_tile, 0, j))],
        out_specs=pl.BlockSpec((tm, tn), lambda i, j: (i, j)),
        out_shape=jax.ShapeDtypeStruct((m, d), F32),
        scratch_shapes=[pltpu.VMEM((tm, d), BF16)],
        compiler_params=_params("arbitrary", "arbitrary"),
        name=name,
    )(o_f, o_b, proj, onorm_g.reshape(1, HEAD_LANES), w_bf, x2d, gate)


def _prefix_sum_rows(x, reverse):
    n = x.shape[0]
    row = lax.broadcasted_iota(jnp.int32, x.shape, 0)
    step = 1
    while step < n:
        if reverse:
            x = x + jnp.where(row < n - step, pltpu.roll(x, n - step, axis=0), 0.0)
        else:
            x = x + jnp.where(row >= step, pltpu.roll(x, step, axis=0), 0.0)
        step *= 2
    return x


def _scan_kernel(*refs, has_s0, emit_o, emit_s):
    refs = list(refs)
    qf_ref, vf_ref, uf_ref, qb_ref, vb_ref, ub_ref, lb_ref = refs[:7]
    refs = refs[7:]
    s0_ref = refs.pop(0) if has_s0 else None
    of_ref, ob_ref = (refs.pop(0), refs.pop(0)) if emit_o else (None, None)
    sout_ref = refs.pop(0) if emit_s else None
    st_ref, = refs
    n = pl.program_id(2)
    tb = qf_ref.shape[1]
    n_chunks = tb // HGRN_CHUNK

    @pl.when(n == 0)
    def _init():
        if has_s0:
            st_ref[...] = s0_ref[:, 0, 0]
        else:
            st_ref[...] = jnp.zeros_like(st_ref)

    tok_r = lax.broadcasted_iota(jnp.int32, (HGRN_CHUNK, HGRN_CHUNK), 0)
    tok_c = lax.broadcasted_iota(jnp.int32, (HGRN_CHUNK, HGRN_CHUNK), 1)
    sources = ((qf_ref, vf_ref, uf_ref, of_ref), (qb_ref, vb_ref, ub_ref, ob_ref))
    for d, (q_ref, v_ref, u_ref, o_ref) in enumerate(sources):
        reverse = d == 1
        lbd = lb_ref[d]
        mask = (tok_c >= tok_r) if reverse else (tok_c <= tok_r)
        order = range(n_chunks - 1, -1, -1) if reverse else range(n_chunks)
        for c in order:
            rows = slice(c * HGRN_CHUNK, (c + 1) * HGRN_CHUNK)
            q = _silu(q_ref[0, rows, :])
            v = v_ref[0, rows, :]
            u = u_ref[0, rows, :]
            logf = jnp.log(lbd + (1.0 - lbd) * jax.nn.sigmoid(u))
            k = (1.0 - lbd) * jax.nn.sigmoid(-u)
            b = _prefix_sum_rows(logf, reverse)
            b_end = b[0:1, :] if reverse else b[HGRN_CHUNK - 1:HGRN_CHUNK, :]
            q_dec = (q * jnp.exp(b)).astype(BF16)
            k_inv = (k * jnp.exp(-b)).astype(BF16)
            k_end = (k * jnp.exp(b_end - b)).astype(BF16)
            v_bf = v.astype(BF16)
            st = st_ref[d]
            if emit_o:
                a = lax.dot_general(q_dec, k_inv, (((1,), (1,)), ((), ())),
                                    preferred_element_type=F32)
                a = jnp.where(mask, a, 0.0).astype(BF16)
                o = (jnp.dot(a, v_bf, preferred_element_type=F32)
                     + lax.dot_general(q_dec, st.astype(BF16), (((1,), (1,)), ((), ())),
                                       preferred_element_type=F32))
                o_ref[0, rows, :] = o
            st_ref[d] = jnp.exp(b_end) * st + lax.dot_general(
                v_bf, k_end, (((0,), (0,)), ((), ())), preferred_element_type=F32)

    if emit_s:
        @pl.when(n == pl.num_programs(2) - 1)
        def _final():
            sout_ref[:, 0, 0] = st_ref[...]


def _scan(proj, lb, s0, *, tb, n_heads, emit_o, emit_s, name):
    bsz, length, _ = proj.shape
    nb = length // tb
    hd = HEAD_LANES
    fwd = lambda off: pl.BlockSpec((1, tb, hd), lambda b, h, n: (b, n, off * n_heads + h))
    bwd = lambda off: pl.BlockSpec((1, tb, hd), lambda b, h, n: (b, nb - 1 - n, off * n_heads + h))
    state = pl.BlockSpec((2, 1, 1, hd, hd), lambda b, h, n: (0, b, h, 0, 0))
    in_specs = [fwd(0), fwd(1), fwd(2), bwd(0), bwd(1), bwd(3),
                pl.BlockSpec((2, 1, hd), lambda b, h, n: (0, 0, h))]
    args = [proj] * 6 + [lb]
    if s0 is not None:
        in_specs.append(state)
        args.append(s0)
    out_specs, out_shape = [], []
    if emit_o:
        out_specs += [pl.BlockSpec((1, tb, hd), lambda b, h, n: (b, n, h)),
                      pl.BlockSpec((1, tb, hd), lambda b, h, n: (b, nb - 1 - n, h))]
        out_shape += [jax.ShapeDtypeStruct((bsz, length, n_heads * hd), F32)] * 2
    if emit_s:
        out_specs.append(state)
        out_shape.append(jax.ShapeDtypeStruct((2, bsz, n_heads, hd, hd), F32))
    kern = functools.partial(_scan_kernel, has_s0=s0 is not None, emit_o=emit_o, emit_s=emit_s)
    return pl.pallas_call(
        kern,
        grid=(bsz, n_heads, nb),
        in_specs=in_specs,
        out_specs=out_specs,
        out_shape=out_shape,
        scratch_shapes=[pltpu.VMEM((2, hd, hd), F32)],
        compiler_params=_params("arbitrary", "arbitrary", "arbitrary"),
        name=name,
    )(*args)


def _rope_tables(length):
    half = HEAD_DIM // 2
    rows = length // GRID_W
    row = jnp.repeat(jnp.arange(rows), GRID_W).astype(F32)
    col = jnp.tile(jnp.arange(GRID_W), rows).astype(F32)
    inv = ROPE_THETA ** (-jnp.arange(0, half, 2, dtype=F32) / half)
    def axis_angles(pos):
        a = pos[:, None] * inv[None, :]
        return jnp.concatenate([a, a], axis=-1)
    ang = jnp.concatenate([axis_angles(row), axis_angles(col)], axis=-1)
    ang = jnp.concatenate([ang, ang], axis=-1)
    first_half = (jnp.arange(HEAD_LANES) % half) < half // 2
    sin = jnp.sin(ang)
    return jnp.cos(ang), jnp.where(first_half, -sin, 0.0), jnp.where(first_half, 0.0, sin)


def kernel(x, c, ctx, c_ctx, w_ada, b_ada, norm_g, w_in_ab, w_out_ab, qn_g, kn_g, lam_q1, lam_k1,
           lam_q2, lam_k2, subln_g, conv_w, conv_b, cln_g, cln_b, w_in_c, w_out_c, lb_gamma, onorm_g):
    bsz, seq, d = x.shape
    n_ctx = ctx.shape[1]
    depth = w_ada.shape[0]
    assert depth == 2 and w_in_ab.shape[0] == 1 and w_in_c.shape[0] == 1
    att_width = N_HEADS * HEAD_LANES
    conv_width = conv_w.shape[-1]
    assert w_in_ab.shape[-1] == 4 * att_width + 3 * conv_width and conv_width == att_width

    cond_rows = jnp.concatenate([c, c_ctx[None, :], jnp.zeros((8 - bsz - 1, d), F32)], axis=0)
    mod = _ada(cond_rows, w_ada, b_ada)
    def mods(l, rows):
        m = mod[l, rows][:, None, :]
        return m[..., :d], m[..., d:2 * d], m[..., 2 * d:]

    cos, sin_a, sin_b = _rope_tables(seq)
    gmat = jnp.kron(jnp.eye(2, dtype=F32), jnp.full((HEAD_DIM, HEAD_DIM), 1.0 / HEAD_DIM, F32)
                    ).astype(BF16)
    two = lambda g: jnp.concatenate([g, g]).reshape(1, HEAD_LANES)
    q_scale = HEAD_DIM ** -0.5

    x2d = x.reshape(bsz * seq, d)
    ctx2d = ctx.reshape(bsz * n_ctx, d)
    lat_rows = slice(0, bsz)
    ctx_rows = slice(bsz, bsz + 1)

    lam_init = 0.8 - 0.6 * math.exp(-0.3 * 0)
    lam = (jnp.exp(jnp.sum(lam_q1[0] * lam_k1[0])) - jnp.exp(jnp.sum(lam_q2[0] * lam_k2[0]))
           + lam_init).reshape(1)
    shift, scale, gate = mods(0, lat_rows)
    shift_c, scale_c, gate_c = mods(0, ctx_rows)
    w_in = w_in_ab[0].astype(BF16)
    w_out = w_out_ab[0].astype(BF16)
    ab_in = w_in.shape[1]
    qk_lat = (two(qn_g[0]), two(kn_g[0]), gmat, cos, sin_a, sin_b, att_width, q_scale)
    qk_ctx = (two(qn_g[0]), two(kn_g[0]), gmat, None, None, None, att_width, q_scale)
    p_lat = _proj(x2d, norm_g[0], scale, shift, w_in, seq, tm=512, tn=512, qk=qk_lat,
                  name="proj_ab_lat").reshape(bsz, seq, ab_in)
    p_ctx = _proj(ctx2d, norm_g[0], scale_c, shift_c, w_in, bsz * n_ctx, tm=512, tn=512, qk=qk_ctx,
                  name="proj_ab_ctx").reshape(bsz, n_ctx, ab_in)
    att_lat = _attention(lam, p_lat, p_ctx, p_lat, subln_g[0], tq=256, att_width=att_width,
                         lam_init=lam_init, name="attn_lat")
    att_ctx = _attention(lam, p_ctx, p_ctx, None, subln_g[0], tq=n_ctx, att_width=att_width,
                         lam_init=lam_init, name="attn_ctx")
    conv_col0 = 4 * att_width // conv_width
    conv_lat = _conv_branch(p_lat, conv_w[0], conv_b[0], cln_g[0], cln_b[0], tl=128,
                            col0=conv_col0, width=conv_width, name="conv_lat")
    conv_ctx = _conv_branch(p_ctx, conv_w[0], conv_b[0], cln_g[0], cln_b[0], tl=128,
                            col0=conv_col0, width=conv_width, name="conv_ctx")
    x2d = _out_ab(att_lat.reshape(bsz * seq, att_width), conv_lat.reshape(bsz * seq, conv_width),
                  w_out, x2d, gate, seq, tm=512, tn=1024, name="out_ab_lat")
    ctx2d = _out_ab(att_ctx.reshape(bsz * n_ctx, att_width),
                    conv_ctx.reshape(bsz * n_ctx, conv_width),
                    w_out, ctx2d, gate_c, bsz * n_ctx, tm=512, tn=1024, name="out_ab_ctx")

    p = jax.nn.softmax(lb_gamma.astype(F32), axis=1)
    lb_all = jnp.cumsum(p, axis=1) - p[:, :1]
    lb = lb_all[:, 1].reshape(2, 1, d)
    shift, scale, gate = mods(1, lat_rows)
    shift_c, scale_c, _ = mods(1, ctx_rows)
    w_in = w_in_c[0].astype(BF16)
    w_out = w_out_c[0].astype(BF16)
    c_in = w_in.shape[1]
    p_lat = _proj(x2d, norm_g[1], scale, shift, w_in, seq, tm=512, tn=512,
                  name="proj_c_lat").reshape(bsz, seq, c_in)
    p_ctx = _proj(ctx2d, norm_g[1], scale_c, shift_c, w_in, bsz * n_ctx, tm=512, tn=512,
                  name="proj_c_ctx").reshape(bsz, n_ctx, c_in)
    s_ctx, = _scan(p_ctx, lb, None, tb=n_ctx, n_heads=HGRN_HEADS, emit_o=False, emit_s=True,
                   name="scan_ctx")
    o_f, o_b = _scan(p_lat, lb, s_ctx, tb=256, n_heads=HGRN_HEADS, emit_o=True, emit_s=False,
                     name="scan_lat")
    out = _out_c(o_f.reshape(bsz * seq, d), o_b.reshape(bsz * seq, d),
                 p_lat.reshape(bsz * seq, c_in), 4, onorm_g[0], w_out, x2d, gate, seq,
                 tm=512, tn=1024, name="out_c_lat")
    return out.reshape(bsz, seq, d)
```

```python
import functools
import math

import jax
import jax.numpy as jnp
from jax import lax
from jax.experimental import pallas as pl
from jax.experimental.pallas import tpu as pltpu

F32 = jnp.float32
BF16 = jnp.bfloat16

EPS = 1e-6
GRID_W = 64
ROPE_THETA = 10000.0
HEAD_DIM = 64
HEAD_LANES = 128
N_HEADS = 8
CONV_K = 31
CONV_HALO = 16
SUBLANES = 8
HGRN_HEADS = 16
HGRN_CHUNK = 64

KEY_CHUNK = 256
MAX_FIXED_SHIFT = 40.0
VMEM_LIMIT = 48 * 1024 * 1024


def _silu(x):
    return x * jax.nn.sigmoid(x)


def _params(*sem):
    return pltpu.CompilerParams(dimension_semantics=sem, vmem_limit_bytes=VMEM_LIMIT)


def _ada_kernel(c_ref, w_ref, b_ref, o_ref):
    sc = _silu(c_ref[...])
    o_ref[0] = jnp.dot(sc, w_ref[0], precision=lax.Precision.HIGHEST,
                       preferred_element_type=F32) + b_ref[0]


def _ada(cond_rows, w_ada, b_ada):
    depth, d, n = w_ada.shape
    tn = 768
    return pl.pallas_call(
        _ada_kernel,
        grid=(depth, n // tn),
        in_specs=[pl.BlockSpec((8, d), lambda l, j: (0, 0)),
                  pl.BlockSpec((1, d, tn), lambda l, j: (l, 0, j)),
                  pl.BlockSpec((1, 1, tn), lambda l, j: (l, 0, j))],
        out_specs=pl.BlockSpec((1, 8, tn), lambda l, j: (l, 0, j)),
        out_shape=jax.ShapeDtypeStruct((depth, 8, n), F32),
        compiler_params=_params("arbitrary", "arbitrary"),
        name="ada",
    )(cond_rows, w_ada, b_ada.reshape(depth, 1, n))


def _group_mean_sq(x, gmat):
    sq = x * x
    hi = sq.astype(BF16)
    lo = (sq - hi.astype(F32)).astype(BF16)
    return (jnp.dot(hi, gmat, preferred_element_type=F32)
            + jnp.dot(lo, gmat, preferred_element_type=F32))


def _proj_kernel(*refs, n_qk_tiles, n_silu_tiles, rope, q_scale, row_chunk, ep_rows):
    if n_qk_tiles:
        (x_ref, g_ref, sc_ref, sh_ref, w_ref, qg_ref, kg_ref, gmat_ref,
         cos_ref, sa_ref, sb_ref, o_ref, xn_ref, acc_ref) = refs
    else:
        x_ref, g_ref, sc_ref, sh_ref, w_ref, o_ref, xn_ref = refs
    j = pl.program_id(1)
    tm, tn = o_ref.shape

    @pl.when(j == 0)
    def _normalise():
        gain = g_ref[...]
        mod = 1.0 + sc_ref[0]
        shift = sh_ref[0]

        def body(r, carry):
            r0 = pl.multiple_of(r * row_chunk, row_chunk)
            x = x_ref[pl.ds(r0, row_chunk), :]
            ms = jnp.mean(x * x, axis=-1, keepdims=True)
            y = x * lax.rsqrt(ms + EPS) * gain
            xn_ref[pl.ds(r0, row_chunk), :] = (y * mod + shift).astype(BF16)
            return carry

        lax.fori_loop(0, tm // row_chunk, body, 0)

    def matmul():
        return jnp.dot(xn_ref[...], w_ref[...], preferred_element_type=F32)

    def norm_rope(gain_ref, scale):
        acc_ref[...] = matmul()
        gmat = gmat_ref[...]
        gain = gain_ref[...]

        def body(r, carry):
            rows = pl.ds(pl.multiple_of(r * ep_rows, ep_rows), ep_rows)
            for c in range(tn // HEAD_LANES):
                cols = slice(c * HEAD_LANES, (c + 1) * HEAD_LANES)
                xs = acc_ref[rows, cols]
                y = xs * lax.rsqrt(_group_mean_sq(xs, gmat) + EPS) * gain
                if rope:
                    y = (y * cos_ref[rows, :]
                         + pltpu.roll(y, HEAD_LANES - 16, axis=1) * sa_ref[rows, :]
                         + pltpu.roll(y, 16, axis=1) * sb_ref[rows, :])
                if scale != 1.0:
                    y = y * scale
                o_ref[rows, cols] = y.astype(BF16)
            return carry

        lax.fori_loop(0, tm // ep_rows, body, 0)

    if n_qk_tiles:
        @pl.when(j < n_qk_tiles)
        def _q():
            norm_rope(qg_ref, q_scale)

        @pl.when((j >= n_qk_tiles) & (j < 2 * n_qk_tiles))
        def _k():
            norm_rope(kg_ref, 1.0)

        @pl.when(j >= 2 * n_qk_tiles)
        def _rest():
            o_ref[...] = matmul().astype(BF16)
    elif n_silu_tiles:
        @pl.when(j < n_silu_tiles)
        def _gated():
            o_ref[...] = _silu(matmul()).astype(BF16)

        @pl.when(j >= n_silu_tiles)
        def _rest():
            o_ref[...] = matmul().astype(BF16)
    else:
        o_ref[...] = matmul().astype(BF16)


def _proj(x2d, gain, scale, shift, w_bf, rows_per_group, *, tm, tn, qk=None, silu_width=0, name):
    m, d = x2d.shape
    n = w_bf.shape[1]
    tiles_per_group = rows_per_group // tm
    in_specs = [
        pl.BlockSpec((tm, d), lambda i, j: (i, 0)),
        pl.BlockSpec((1, d), lambda i, j: (0, 0)),
        pl.BlockSpec((1, 1, d), lambda i, j: (i // tiles_per_group, 0, 0)),
        pl.BlockSpec((1, 1, d), lambda i, j: (i // tiles_per_group, 0, 0)),
        pl.BlockSpec((d, tn), lambda i, j: (0, j)),
    ]
    args = [x2d, gain.reshape(1, d), scale, shift, w_bf]
    scratch = [pltpu.VMEM((tm, d), BF16)]
    n_qk_tiles, rope, q_scale = 0, False, 1.0
    if qk is not None:
        qg, kg, gmat, cos, sa, sb, q_width, q_scale = qk
        n_qk_tiles = q_width // tn
        rope = cos is not None
        const = lambda i, j: (0, 0)
        in_specs += [pl.BlockSpec((1, HEAD_LANES), const), pl.BlockSpec((1, HEAD_LANES), const),
                     pl.BlockSpec((HEAD_LANES, HEAD_LANES), const)]
        args += [qg, kg, gmat]
        if rope:
            pos_tiles = cos.shape[0] // tm
            pos = lambda i, j: (i % pos_tiles, 0)
            in_specs += [pl.BlockSpec((tm, HEAD_LANES), pos)] * 3
            args += [cos, sa, sb]
        else:
            dummy = jnp.zeros((8, HEAD_LANES), F32)
            in_specs += [pl.BlockSpec((8, HEAD_LANES), const)] * 3
            args += [dummy, dummy, dummy]
        scratch.append(pltpu.VMEM((tm, tn), F32))
    kern = functools.partial(_proj_kernel, n_qk_tiles=n_qk_tiles, n_silu_tiles=silu_width // tn,
                             rope=rope, q_scale=q_scale, row_chunk=64, ep_rows=256)
    return pl.pallas_call(
        kern,
        grid=(m // tm, n // tn),
        in_specs=in_specs,
        out_specs=pl.BlockSpec((tm, tn), lambda i, j: (i, j)),
        out_shape=jax.ShapeDtypeStruct((m, n), BF16),
        scratch_shapes=scratch,
        compiler_params=_params("arbitrary", "arbitrary"),
        name=name,
    )(*args)


def _stage_keys_values(kc_ref, vc_ref, kl_ref, vl_ref, kb_ref, vt_ref, n_ctx_chunks, n_lat_chunks):
    def transposed(v):
        return v.astype(F32).T.astype(BF16)

    for c in range(n_ctx_chunks):
        rows = slice(c * KEY_CHUNK, (c + 1) * KEY_CHUNK)
        kb_ref[c] = kc_ref[0, rows, :]
        vt_ref[c] = transposed(vc_ref[0, rows, :])
    if n_lat_chunks:
        def body(c, carry):
            rows = pl.ds(pl.multiple_of(c * KEY_CHUNK, KEY_CHUNK), KEY_CHUNK)
            kb_ref[n_ctx_chunks + c] = kl_ref[0, rows, :]
            vt_ref[n_ctx_chunks + c] = transposed(vl_ref[0, rows, :])
            return carry
        lax.fori_loop(0, n_lat_chunks, body, 0)


def _attn_refs(refs, n_lat_chunks):
    if n_lat_chunks:
        return refs
    scal_ref, q_ref, kc_ref, vc_ref, za_ref, sg_ref, o_ref, kb_ref, vt_ref = refs
    return scal_ref, q_ref, kc_ref, vc_ref, None, None, za_ref, sg_ref, o_ref, kb_ref, vt_ref


def _query_maps(q_ref):
    q = q_ref[0]
    lane = lax.broadcasted_iota(jnp.int32, q.shape, 1)
    zero = jnp.zeros_like(q)
    return jnp.where(lane < HEAD_DIM, q, zero), jnp.where(lane >= HEAD_DIM, q, zero)


def _attn_finish(o_t, za_ref, sg_ref, o_ref, lam_init):
    o = o_t.T
    ms = jnp.mean(o * o, axis=-1, keepdims=True)
    y = o * lax.rsqrt(ms + EPS) * sg_ref[...] * (1.0 - lam_init)
    o_ref[0] = (y * _silu(za_ref[0].astype(F32))).astype(BF16)


def _attn_kernel(*refs, n_ctx_chunks, n_lat_chunks, lam_init):
    (scal_ref, q_ref, kc_ref, vc_ref, kl_ref, vl_ref, za_ref, sg_ref,
     o_ref, kb_ref, vt_ref) = _attn_refs(refs, n_lat_chunks)
    tq = q_ref.shape[1]
    n_chunks = n_ctx_chunks + n_lat_chunks

    @pl.when(pl.program_id(2) == 0)
    def _stage():
        _stage_keys_values(kc_ref, vc_ref, kl_ref, vl_ref, kb_ref, vt_ref, n_ctx_chunks, n_lat_chunks)

    q_maps = _query_maps(q_ref)

    def body(c, carry):
        kblk = kb_ref[c]
        vtb = vt_ref[c]
        out = []
        for mp in range(2):
            m_old, l_old, o_old = carry[3 * mp:3 * mp + 3]
            s = lax.dot_general(kblk, q_maps[mp], (((1,), (1,)), ((), ())),
                                preferred_element_type=F32)
            m_new = jnp.maximum(m_old, jnp.max(s, axis=0, keepdims=True))
            p = jnp.exp(s - m_new)
            alpha = jnp.exp(m_old - m_new)
            l_new = alpha * l_old + jnp.sum(p, axis=0, keepdims=True)
            o_new = alpha * o_old + jnp.dot(vtb, p.astype(BF16), preferred_element_type=F32)
            out += [m_new, l_new, o_new]
        return tuple(out)

    m0 = jnp.full((1, tq), -1e30, F32)
    l0 = jnp.zeros((1, tq), F32)
    o0 = jnp.zeros((HEAD_LANES, tq), F32)
    _, l_a, o_a, _, l_b, o_b = lax.fori_loop(0, n_chunks, body, (m0, l0, o0, m0, l0, o0))
    _attn_finish(o_a / l_a - scal_ref[0] * (o_b / l_b), za_ref, sg_ref, o_ref, lam_init)


def _attn_bounded_kernel(*refs, n_ctx_chunks, n_lat_chunks, lam_init):
    (scal_ref, q_ref, kc_ref, vc_ref, kl_ref, vl_ref, za_ref, sg_ref,
     o_ref, kb_ref, vt_ref) = _attn_refs(refs, n_lat_chunks)
    tq = q_ref.shape[1]
    n_chunks = n_ctx_chunks + n_lat_chunks

    @pl.when(pl.program_id(2) == 0)
    def _stage():
        _stage_keys_values(kc_ref, vc_ref, kl_ref, vl_ref, kb_ref, vt_ref, n_ctx_chunks, n_lat_chunks)

    shift = scal_ref[1]
    q_maps = _query_maps(q_ref)
    l_acc = [jnp.zeros((SUBLANES, tq), F32), jnp.zeros((SUBLANES, tq), F32)]
    o_acc = [jnp.zeros((HEAD_LANES, tq), F32), jnp.zeros((HEAD_LANES, tq), F32)]
    for c in range(n_chunks):
        kblk = kb_ref[c]
        vtb = vt_ref[c]
        for mp in range(2):
            s = lax.dot_general(kblk, q_maps[mp], (((1,), (1,)), ((), ())),
                                preferred_element_type=F32)
            p = jnp.exp(s - shift)
            l_acc[mp] = l_acc[mp] + jnp.sum(p.reshape(KEY_CHUNK // SUBLANES, SUBLANES, tq), axis=0)
            o_acc[mp] = o_acc[mp] + jnp.dot(vtb, p.astype(BF16), preferred_element_type=F32)

    l_a = jnp.sum(l_acc[0], axis=0, keepdims=True)
    l_b = jnp.sum(l_acc[1], axis=0, keepdims=True)
    _attn_finish(o_acc[0] / l_a - scal_ref[0] * (o_acc[1] / l_b), za_ref, sg_ref, o_ref, lam_init)


def _attention(scal, proj_q, proj_ctx, proj_lat, subln_g, *, tq, att_width, lam_init, bounded, name):
    bsz, lq, _ = proj_q.shape
    n_ctx = proj_ctx.shape[1]
    nh = att_width // HEAD_LANES
    n_ctx_chunks = n_ctx // KEY_CHUNK
    n_lat_chunks = 0 if proj_lat is None else proj_lat.shape[1] // KEY_CHUNK
    k_off, v_off, z_off = nh, 2 * nh, 3 * nh
    in_specs = [
        pl.BlockSpec(memory_space=pltpu.SMEM),
        pl.BlockSpec((1, tq, HEAD_LANES), lambda b, h, i: (b, i, h)),
        pl.BlockSpec((1, n_ctx, HEAD_LANES), lambda b, h, i: (b, 0, k_off + h)),
        pl.BlockSpec((1, n_ctx, HEAD_LANES), lambda b, h, i: (b, 0, v_off + h)),
    ]
    args = [scal, proj_q, proj_ctx, proj_ctx]
    if n_lat_chunks:
        n_lat = proj_lat.shape[1]
        in_specs += [pl.BlockSpec((1, n_lat, HEAD_LANES), lambda b, h, i: (b, 0, k_off + h)),
                     pl.BlockSpec((1, n_lat, HEAD_LANES), lambda b, h, i: (b, 0, v_off + h))]
        args += [proj_lat, proj_lat]
    in_specs += [pl.BlockSpec((1, tq, HEAD_LANES), lambda b, h, i: (b, i, z_off + h)),
                 pl.BlockSpec((1, HEAD_LANES), lambda b, h, i: (0, 0))]
    args += [proj_q, subln_g.reshape(1, HEAD_LANES)]
    n_chunks = n_ctx_chunks + n_lat_chunks
    kern = functools.partial(_attn_bounded_kernel if bounded else _attn_kernel,
                             n_ctx_chunks=n_ctx_chunks, n_lat_chunks=n_lat_chunks, lam_init=lam_init)
    return pl.pallas_call(
        kern,
        grid=(bsz, nh, lq // tq),
        in_specs=in_specs,
        out_specs=pl.BlockSpec((1, tq, HEAD_LANES), lambda b, h, i: (b, i, h)),
        out_shape=jax.ShapeDtypeStruct((bsz, lq, att_width), BF16),
        scratch_shapes=[pltpu.VMEM((n_chunks, KEY_CHUNK, HEAD_LANES), BF16),
                        pltpu.VMEM((n_chunks, HEAD_LANES, KEY_CHUNK), BF16)],
        compiler_params=_params("arbitrary", "arbitrary", "arbitrary"),
        name=name,
    )(*args)


def _conv_kernel(gv_ref, gg_ref, zb_ref, gvp_ref, ggp_ref, gvn_ref, ggn_ref,
                 w_ref, b_ref, lg_ref, lb_ref, o_ref, ypad_ref, ysh_ref, *, row_chunk):
    i = pl.program_id(1)
    n_tiles = pl.num_programs(1)
    tl = o_ref.shape[1]
    padded = tl + 2 * CONV_HALO

    def glu(val_ref, gate_ref, rows):
        return val_ref[0, rows, :].astype(F32) * jax.nn.sigmoid(gate_ref[0, rows, :].astype(F32))

    halo = slice(0, CONV_HALO)
    ypad_ref[0:CONV_HALO, :] = jnp.where(i > 0, glu(gvp_ref, ggp_ref, halo), 0.0)
    ypad_ref[CONV_HALO + tl:padded, :] = jnp.where(i < n_tiles - 1, glu(gvn_ref, ggn_ref, halo), 0.0)
    for r in range(tl // row_chunk):
        rows = slice(r * row_chunk, (r + 1) * row_chunk)
        ypad_ref[CONV_HALO + r * row_chunk:CONV_HALO + (r + 1) * row_chunk, :] = glu(gv_ref, gg_ref, rows)

    span = padded - SUBLANES
    for s in range(1, SUBLANES):
        for r in range(0, span, row_chunk):
            n = min(row_chunk, span - r)
            ysh_ref[s, r:r + n, :] = ypad_ref[r + s:r + s + n, :]

    first = CONV_HALO - CONV_K // 2
    for r in range(tl // row_chunk):
        acc = None
        for t in range(CONV_K):
            start = r * row_chunk + first + t
            phase, base = start % SUBLANES, start - start % SUBLANES
            src = ypad_ref[base:base + row_chunk, :] if phase == 0 else ysh_ref[phase, base:base + row_chunk, :]
            term = w_ref[t:t + 1, :] * src
            acc = term + b_ref[...] if acc is None else acc + term
        mu = jnp.mean(acc, axis=-1, keepdims=True)
        xc = acc - mu
        var = jnp.mean(xc * xc, axis=-1, keepdims=True)
        y = xc * lax.rsqrt(var + EPS) * lg_ref[...] + lb_ref[...]
        rows = slice(r * row_chunk, (r + 1) * row_chunk)
        o_ref[0, rows, :] = (_silu(y) * _silu(zb_ref[0, rows, :].astype(F32))).astype(BF16)


def _conv_branch(proj, conv_w, conv_b, cln_g, cln_b, *, tl, col0, width, name):
    bsz, length, _ = proj.shape
    n_tiles = length // tl
    halo_per_tile = tl // CONV_HALO
    last_halo = length // CONV_HALO - 1
    main = lambda off: pl.BlockSpec((1, tl, width), lambda b, i: (b, i, col0 + off))
    prev = lambda off: pl.BlockSpec(
        (1, CONV_HALO, width), lambda b, i: (b, jnp.maximum(i * halo_per_tile - 1, 0), col0 + off))
    nxt = lambda off: pl.BlockSpec(
        (1, CONV_HALO, width),
        lambda b, i: (b, jnp.minimum((i + 1) * halo_per_tile, last_halo), col0 + off))
    row = lambda k: pl.BlockSpec((k, width), lambda b, i: (0, 0))
    padded = tl + 2 * CONV_HALO
    return pl.pallas_call(
        functools.partial(_conv_kernel, row_chunk=32),
        grid=(bsz, n_tiles),
        in_specs=[main(0), main(1), main(2), prev(0), prev(1), nxt(0), nxt(1),
                  row(CONV_K), row(1), row(1), row(1)],
        out_specs=pl.BlockSpec((1, tl, width), lambda b, i: (b, i, 0)),
        out_shape=jax.ShapeDtypeStruct((bsz, length, width), BF16),
        scratch_shapes=[pltpu.VMEM((padded, width), F32),
                        pltpu.VMEM((SUBLANES, padded, width), F32)],
        compiler_params=_params("arbitrary", "arbitrary"),
        name=name,
    )(proj, proj, proj, proj, proj, proj, proj,
      conv_w, conv_b.reshape(1, width), cln_g.reshape(1, width), cln_b.reshape(1, width))


def _out_ab_kernel(a_ref, c_ref, wa_ref, wc_ref, x_ref, gate_ref, o_ref):
    y = (jnp.dot(a_ref[...], wa_ref[...], preferred_element_type=F32)
         + jnp.dot(c_ref[...], wc_ref[...], preferred_element_type=F32))
    o_ref[...] = x_ref[...] + gate_ref[0] * y


def _out_ab(att, conv, w_bf, x2d, gate, rows_per_group, *, tm, tn, name):
    m, d = x2d.shape
    ka = att.shape[1]
    kc = conv.shape[1]
    tiles_per_group = rows_per_group // tm
    return pl.pallas_call(
        _out_ab_kernel,
        grid=(m // tm, d // tn),
        in_specs=[pl.BlockSpec((tm, ka), lambda i, j: (i, 0)),
                  pl.BlockSpec((tm, kc), lambda i, j: (i, 0)),
                  pl.BlockSpec((ka, tn), lambda i, j: (0, j)),
                  pl.BlockSpec((kc, tn), lambda i, j: (ka // kc, j)),
                  pl.BlockSpec((tm, tn), lambda i, j: (i, j)),
                  pl.BlockSpec((1, 1, tn), lambda i, j: (i // tiles_per_group, 0, j))],
        out_specs=pl.BlockSpec((tm, tn), lambda i, j: (i, j)),
        out_shape=jax.ShapeDtypeStruct((m, d), F32),
        compiler_params=_params("arbitrary", "arbitrary"),
        name=name,
    )(att, conv, w_bf, w_bf, x2d, gate)


def _out_c_kernel(of_ref, ob_ref, z_ref, g_ref, w_ref, x_ref, gate_ref, o_ref, y_ref, *, row_chunk):
    j = pl.program_id(1)
    tm, d = y_ref.shape

    @pl.when(j == 0)
    def _readout():
        def body(r, carry):
            rows = pl.ds(pl.multiple_of(r * row_chunk, row_chunk), row_chunk)
            for h in range(d // HEAD_LANES):
                cols = slice(h * HEAD_LANES, (h + 1) * HEAD_LANES)
                o = of_ref[rows, cols].astype(F32) + ob_ref[rows, cols].astype(F32)
                ms = jnp.mean(o * o, axis=-1, keepdims=True)
                y = o * lax.rsqrt(ms + EPS) * g_ref[...]
                y_ref[rows, cols] = (y * _silu(z_ref[rows, cols].astype(F32))).astype(BF16)
            return carry
        lax.fori_loop(0, tm // row_chunk, body, 0)

    y = jnp.dot(y_ref[...], w_ref[...], preferred_element_type=F32)
    o_ref[...] = x_ref[...] + gate_ref[0] * y


def _out_c(o_f, o_b, proj, z_col, onorm_g, w_bf, x2d, gate, rows_per_group, *, tm, tn, name):
    m, d = x2d.shape
    tiles_per_group = rows_per_group // tm
    return pl.pallas_call(
        functools.partial(_out_c_kernel, row_chunk=64),
        grid=(m // tm, d // tn),
        in_specs=[pl.BlockSpec((tm, d), lambda i, j: (i, 0)),
                  pl.BlockSpec((tm, d), lambda i, j: (i, 0)),
                  pl.BlockSpec((tm, d), lambda i, j: (i, z_col)),
                  pl.BlockSpec((1, HEAD_LANES), lambda i, j: (0, 0)),
                  pl.BlockSpec((d, tn), lambda i, j: (0, j)),
                  pl.BlockSpec((tm, tn), lambda i, j: (i, j)),
                  pl.BlockSpec((1, 1, tn), lambda i, j: (i // tiles_per_group, 0, j))],
        out_specs=pl.BlockSpec((tm, tn), lambda i, j: (i, j)),
        out_shape=jax.ShapeDtypeStruct((m, d), F32),
        scratch_shapes=[pltpu.VMEM((tm, d), BF16)],
        compiler_params=_params("arbitrary", "arbitrary"),
        name=name,
    )(o_f, o_b, proj, onorm_g.reshape(1, HEAD_LANES), w_bf, x2d, gate)


def _prefix_sum_rows(x, reverse):
    n = x.shape[0]
    row = lax.broadcasted_iota(jnp.int32, x.shape, 0)
    step = 1
    while step < n:
        if reverse:
            x = x + jnp.where(row < n - step, pltpu.roll(x, n - step, axis=0), 0.0)
        else:
            x = x + jnp.where(row >= step, pltpu.roll(x, step, axis=0), 0.0)
        step *= 2
    return x


def _scan_kernel(*refs, has_s0, emit_o, emit_s):
    refs = list(refs)
    qf_ref, vf_ref, uf_ref, qb_ref, vb_ref, ub_ref, lb_ref = refs[:7]
    refs = refs[7:]
    s0_ref = refs.pop(0) if has_s0 else None
    of_ref, ob_ref = (refs.pop(0), refs.pop(0)) if emit_o else (None, None)
    sout_ref = refs.pop(0) if emit_s else None
    st_ref, = refs
    n = pl.program_id(2)
    tb = qf_ref.shape[1]
    n_chunks = tb // HGRN_CHUNK

    @pl.when(n == 0)
    def _init():
        if has_s0:
            st_ref[...] = s0_ref[:, 0, 0]
        else:
            st_ref[...] = jnp.zeros_like(st_ref)

    tok_r = lax.broadcasted_iota(jnp.int32, (HGRN_CHUNK, HGRN_CHUNK), 0)
    tok_c = lax.broadcasted_iota(jnp.int32, (HGRN_CHUNK, HGRN_CHUNK), 1)
    sources = ((qf_ref, vf_ref, uf_ref, of_ref), (qb_ref, vb_ref, ub_ref, ob_ref))
    for d, (q_ref, v_ref, u_ref, o_ref) in enumerate(sources):
        reverse = d == 1
        lbd = lb_ref[d]
        mask = (tok_c >= tok_r) if reverse else (tok_c <= tok_r)
        order = range(n_chunks - 1, -1, -1) if reverse else range(n_chunks)
        for c in order:
            rows = slice(c * HGRN_CHUNK, (c + 1) * HGRN_CHUNK)
            q = q_ref[0, rows, :].astype(F32)
            v_bf = v_ref[0, rows, :]
            u = u_ref[0, rows, :].astype(F32)
            sig = jax.nn.sigmoid(u)
            logf = jnp.log(lbd + (1.0 - lbd) * sig)
            k = (1.0 - lbd) * (1.0 - sig)
            b = _prefix_sum_rows(logf, reverse)
            b_end = b[0:1, :] if reverse else b[HGRN_CHUNK - 1:HGRN_CHUNK, :]
            decay = jnp.exp(b)
            k_inv = k / decay
            decay_end = jnp.exp(b_end)
            q_dec = (q * decay).astype(BF16)
            k_end = (k_inv * decay_end).astype(BF16)
            k_inv = k_inv.astype(BF16)
            st = st_ref[d]
            if emit_o:
                a = lax.dot_general(q_dec, k_inv, (((1,), (1,)), ((), ())),
                                    preferred_element_type=F32)
                a = jnp.where(mask, a, 0.0).astype(BF16)
                o = (jnp.dot(a, v_bf, preferred_element_type=F32)
                     + lax.dot_general(q_dec, st.astype(BF16), (((1,), (1,)), ((), ())),
                                       preferred_element_type=F32))
                o_ref[0, rows, :] = o.astype(BF16)
            st_ref[d] = decay_end * st + lax.dot_general(
                v_bf, k_end, (((0,), (0,)), ((), ())), preferred_element_type=F32)

    if emit_s:
        @pl.when(n == pl.num_programs(2) - 1)
        def _final():
            sout_ref[:, 0, 0] = st_ref[...]


def _scan(proj, lb, s0, *, tb, n_heads, emit_o, emit_s, name):
    bsz, length, _ = proj.shape
    nb = length // tb
    hd = HEAD_LANES
    fwd = lambda off: pl.BlockSpec((1, tb, hd), lambda b, h, n: (b, n, off * n_heads + h))
    bwd = lambda off: pl.BlockSpec((1, tb, hd), lambda b, h, n: (b, nb - 1 - n, off * n_heads + h))
    state = pl.BlockSpec((2, 1, 1, hd, hd), lambda b, h, n: (0, b, h, 0, 0))
    in_specs = [fwd(0), fwd(1), fwd(2), bwd(0), bwd(1), bwd(3),
                pl.BlockSpec((2, 1, hd), lambda b, h, n: (0, 0, h))]
    args = [proj] * 6 + [lb]
    if s0 is not None:
        in_specs.append(state)
        args.append(s0)
    out_specs, out_shape = [], []
    if emit_o:
        out_specs += [pl.BlockSpec((1, tb, hd), lambda b, h, n: (b, n, h)),
                      pl.BlockSpec((1, tb, hd), lambda b, h, n: (b, nb - 1 - n, h))]
        out_shape += [jax.ShapeDtypeStruct((bsz, length, n_heads * hd), BF16)] * 2
    if emit_s:
        out_specs.append(state)
        out_shape.append(jax.ShapeDtypeStruct((2, bsz, n_heads, hd, hd), F32))
    kern = functools.partial(_scan_kernel, has_s0=s0 is not None, emit_o=emit_o, emit_s=emit_s)
    return pl.pallas_call(
        kern,
        grid=(bsz, n_heads, nb),
        in_specs=in_specs,
        out_specs=out_specs,
        out_shape=out_shape,
        scratch_shapes=[pltpu.VMEM((2, hd, hd), F32)],
        compiler_params=_params("arbitrary", "arbitrary", "arbitrary"),
        name=name,
    )(*args)


def _rope_tables(length):
    half = HEAD_DIM // 2
    rows = length // GRID_W
    row = jnp.repeat(jnp.arange(rows), GRID_W).astype(F32)
    col = jnp.tile(jnp.arange(GRID_W), rows).astype(F32)
    inv = ROPE_THETA ** (-jnp.arange(0, half, 2, dtype=F32) / half)
    def axis_angles(pos):
        a = pos[:, None] * inv[None, :]
        return jnp.concatenate([a, a], axis=-1)
    ang = jnp.concatenate([axis_angles(row), axis_angles(col)], axis=-1)
    ang = jnp.concatenate([ang, ang], axis=-1)
    first_half = (jnp.arange(HEAD_LANES) % half) < half // 2
    sin = jnp.sin(ang)
    return jnp.cos(ang), jnp.where(first_half, -sin, 0.0), jnp.where(first_half, 0.0, sin)


def kernel(x, c, ctx, c_ctx, w_ada, b_ada, norm_g, w_in_ab, w_out_ab, qn_g, kn_g, lam_q1, lam_k1,
           lam_q2, lam_k2, subln_g, conv_w, conv_b, cln_g, cln_b, w_in_c, w_out_c, lb_gamma, onorm_g):
    bsz, seq, d = x.shape
    n_ctx = ctx.shape[1]
    depth = w_ada.shape[0]
    assert depth == 2 and w_in_ab.shape[0] == 1 and w_in_c.shape[0] == 1
    att_width = N_HEADS * HEAD_LANES
    conv_width = conv_w.shape[-1]
    assert w_in_ab.shape[-1] == 4 * att_width + 3 * conv_width and conv_width == att_width

    cond_rows = jnp.concatenate([c, c_ctx[None, :], jnp.zeros((8 - bsz - 1, d), F32)], axis=0)
    mod = _ada(cond_rows, w_ada, b_ada)
    def mods(l, rows):
        m = mod[l, rows][:, None, :]
        return m[..., :d], m[..., d:2 * d], m[..., 2 * d:]

    cos, sin_a, sin_b = _rope_tables(seq)
    gmat = jnp.kron(jnp.eye(2, dtype=F32), jnp.full((HEAD_DIM, HEAD_DIM), 1.0 / HEAD_DIM, F32)
                    ).astype(BF16)
    two = lambda g: jnp.concatenate([g, g]).reshape(1, HEAD_LANES)
    q_scale = HEAD_DIM ** -0.5

    x2d = x.reshape(bsz * seq, d)
    ctx2d = ctx.reshape(bsz * n_ctx, d)
    lat_rows = slice(0, bsz)
    ctx_rows = slice(bsz, bsz + 1)

    lam_init = 0.8 - 0.6 * math.exp(-0.3 * 0)
    lam = (jnp.exp(jnp.sum(lam_q1[0] * lam_k1[0])) - jnp.exp(jnp.sum(lam_q2[0] * lam_k2[0]))
           + lam_init).reshape(1)
    shift, scale, gate = mods(0, lat_rows)
    shift_c, scale_c, gate_c = mods(0, ctx_rows)
    w_in = w_in_ab[0].astype(BF16)
    w_out = w_out_ab[0].astype(BF16)
    ab_in = w_in.shape[1]
    qk_lat = (two(qn_g[0]), two(kn_g[0]), gmat, cos, sin_a, sin_b, att_width, q_scale)
    qk_ctx = (two(qn_g[0]), two(kn_g[0]), gmat, None, None, None, att_width, q_scale)
    p_lat = _proj(x2d, norm_g[0], scale, shift, w_in, seq, tm=1024, tn=512, qk=qk_lat,
                  name="proj_ab_lat").reshape(bsz, seq, ab_in)
    p_ctx = _proj(ctx2d, norm_g[0], scale_c, shift_c, w_in, bsz * n_ctx, tm=1024, tn=512, qk=qk_ctx,
                  name="proj_ab_ctx").reshape(bsz, n_ctx, ab_in)
    bound = math.sqrt(HEAD_DIM) * jnp.max(jnp.abs(qn_g[0])) * jnp.max(jnp.abs(kn_g[0]))
    scal = jnp.concatenate([lam, bound.reshape(1)])
    def attend(bounded):
        def run():
            lat = _attention(scal, p_lat, p_ctx, p_lat, subln_g[0], tq=256, att_width=att_width,
                             lam_init=lam_init, bounded=bounded,
                             name="attn_lat_bounded" if bounded else "attn_lat")
            ctx_ = _attention(scal, p_ctx, p_ctx, None, subln_g[0], tq=n_ctx, att_width=att_width,
                              lam_init=lam_init, bounded=bounded,
                              name="attn_ctx_bounded" if bounded else "attn_ctx")
            return lat, ctx_
        return run
    att_lat, att_ctx = lax.cond(bound <= MAX_FIXED_SHIFT, attend(True), attend(False))
    conv_col0 = 4 * att_width // conv_width
    conv_lat = _conv_branch(p_lat, conv_w[0], conv_b[0], cln_g[0], cln_b[0], tl=256,
                            col0=conv_col0, width=conv_width, name="conv_lat")
    conv_ctx = _conv_branch(p_ctx, conv_w[0], conv_b[0], cln_g[0], cln_b[0], tl=256,
                            col0=conv_col0, width=conv_width, name="conv_ctx")
    x2d = _out_ab(att_lat.reshape(bsz * seq, att_width), conv_lat.reshape(bsz * seq, conv_width),
                  w_out, x2d, gate, seq, tm=512, tn=1024, name="out_ab_lat")
    ctx2d = _out_ab(att_ctx.reshape(bsz * n_ctx, att_width),
                    conv_ctx.reshape(bsz * n_ctx, conv_width),
                    w_out, ctx2d, gate_c, bsz * n_ctx, tm=512, tn=1024, name="out_ab_ctx")

    p = jax.nn.softmax(lb_gamma.astype(F32), axis=1)
    lb_all = jnp.cumsum(p, axis=1) - p[:, :1]
    lb = lb_all[:, 1].reshape(2, 1, d)
    shift, scale, gate = mods(1, lat_rows)
    shift_c, scale_c, _ = mods(1, ctx_rows)
    w_in = w_in_c[0].astype(BF16)
    w_out = w_out_c[0].astype(BF16)
    c_in = w_in.shape[1]
    p_lat = _proj(x2d, norm_g[1], scale, shift, w_in, seq, tm=1024, tn=512, silu_width=d,
                  name="proj_c_lat").reshape(bsz, seq, c_in)
    p_ctx = _proj(ctx2d, norm_g[1], scale_c, shift_c, w_in, bsz * n_ctx, tm=1024, tn=512,
                  silu_width=d, name="proj_c_ctx").reshape(bsz, n_ctx, c_in)
    s_ctx, = _scan(p_ctx, lb, None, tb=n_ctx, n_heads=HGRN_HEADS, emit_o=False, emit_s=True,
                   name="scan_ctx")
    o_f, o_b = _scan(p_lat, lb, s_ctx, tb=256, n_heads=HGRN_HEADS, emit_o=True, emit_s=False,
                     name="scan_lat")
    out = _out_c(o_f.reshape(bsz * seq, d), o_b.reshape(bsz * seq, d),
                 p_lat.reshape(bsz * seq, c_in), 4, onorm_g[0], w_out, x2d, gate, seq,
                 tm=512, tn=1024, name="out_c_lat")
    return out.reshape(bsz, seq, d)
```

```python
import functools
import math

import jax
import jax.numpy as jnp
from jax import lax
from jax.experimental import pallas as pl
from jax.experimental.pallas import tpu as pltpu

F32 = jnp.float32
BF16 = jnp.bfloat16

EPS = 1e-6
GRID_W = 64
ROPE_THETA = 10000.0
HEAD_DIM = 64
HEAD_LANES = 128
N_HEADS = 8
CONV_K = 31
CONV_HALO = 16
SUBLANES = 8
HGRN_HEADS = 16
HGRN_CHUNK = 64
SCAN_BLOCK = 256

KEY_CHUNK = 256
Q_SUB = 256
MAX_FIXED_SHIFT = 40.0
VMEM_LIMIT = 48 * 1024 * 1024


def _silu(x):
    return x * jax.nn.sigmoid(x)


def _params(*sem):
    return pltpu.CompilerParams(dimension_semantics=sem, vmem_limit_bytes=VMEM_LIMIT)


def _ada_kernel(c_ref, w_ref, b_ref, o_ref):
    sc = _silu(c_ref[...])
    o_ref[0] = jnp.dot(sc, w_ref[0], precision=lax.Precision.HIGHEST,
                       preferred_element_type=F32) + b_ref[0]


def _ada(cond_rows, w_ada, b_ada):
    depth, d, n = w_ada.shape
    tn = 768
    return pl.pallas_call(
        _ada_kernel,
        grid=(depth, n // tn),
        in_specs=[pl.BlockSpec((8, d), lambda l, j: (0, 0)),
                  pl.BlockSpec((1, d, tn), lambda l, j: (l, 0, j)),
                  pl.BlockSpec((1, 1, tn), lambda l, j: (l, 0, j))],
        out_specs=pl.BlockSpec((1, 8, tn), lambda l, j: (l, 0, j)),
        out_shape=jax.ShapeDtypeStruct((depth, 8, n), F32),
        compiler_params=_params("arbitrary", "arbitrary"),
        name="ada",
    )(cond_rows, w_ada, b_ada.reshape(depth, 1, n))


def _group_mean_sq(x, gmat):
    sq = x * x
    hi = sq.astype(BF16)
    lo = (sq - hi.astype(F32)).astype(BF16)
    return (jnp.dot(hi, gmat, preferred_element_type=F32)
            + jnp.dot(lo, gmat, preferred_element_type=F32))


def _proj_kernel(*refs, n_qk_tiles, n_silu_tiles, rope, q_scale, row_chunk, ep_rows):
    if n_qk_tiles:
        (x_ref, g_ref, sc_ref, sh_ref, w_ref, qg_ref, kg_ref, gmat_ref,
         cos_ref, sa_ref, sb_ref, o_ref, xn_ref, acc_ref) = refs
    else:
        x_ref, g_ref, sc_ref, sh_ref, w_ref, o_ref, xn_ref = refs
    j = pl.program_id(1)
    tm, tn = o_ref.shape

    @pl.when(j == 0)
    def _normalise():
        gain = g_ref[...]
        mod = 1.0 + sc_ref[0]
        shift = sh_ref[0]

        def body(r, carry):
            r0 = pl.multiple_of(r * row_chunk, row_chunk)
            x = x_ref[pl.ds(r0, row_chunk), :]
            ms = jnp.mean(x * x, axis=-1, keepdims=True)
            y = x * lax.rsqrt(ms + EPS) * gain
            xn_ref[pl.ds(r0, row_chunk), :] = (y * mod + shift).astype(BF16)
            return carry

        lax.fori_loop(0, tm // row_chunk, body, 0)

    def matmul():
        return jnp.dot(xn_ref[...], w_ref[...], preferred_element_type=F32)

    def norm_rope(gain_ref, scale):
        acc_ref[...] = matmul()
        gmat = gmat_ref[...]
        gain = gain_ref[...]

        def body(r, carry):
            rows = pl.ds(pl.multiple_of(r * ep_rows, ep_rows), ep_rows)
            for c in range(tn // HEAD_LANES):
                cols = slice(c * HEAD_LANES, (c + 1) * HEAD_LANES)
                xs = acc_ref[rows, cols]
                y = xs * lax.rsqrt(_group_mean_sq(xs, gmat) + EPS) * gain
                if rope:
                    y = (y * cos_ref[rows, :]
                         + pltpu.roll(y, HEAD_LANES - 16, axis=1) * sa_ref[rows, :]
                         + pltpu.roll(y, 16, axis=1) * sb_ref[rows, :])
                if scale != 1.0:
                    y = y * scale
                o_ref[rows, cols] = y.astype(BF16)
            return carry

        lax.fori_loop(0, tm // ep_rows, body, 0)

    if n_qk_tiles:
        @pl.when(j < n_qk_tiles)
        def _q():
            norm_rope(qg_ref, q_scale)

        @pl.when((j >= n_qk_tiles) & (j < 2 * n_qk_tiles))
        def _k():
            norm_rope(kg_ref, 1.0)

        @pl.when(j >= 2 * n_qk_tiles)
        def _rest():
            o_ref[...] = matmul().astype(BF16)
    elif n_silu_tiles:
        @pl.when(j < n_silu_tiles)
        def _gated():
            o_ref[...] = _silu(matmul()).astype(BF16)

        @pl.when(j >= n_silu_tiles)
        def _rest():
            o_ref[...] = matmul().astype(BF16)
    else:
        o_ref[...] = matmul().astype(BF16)


def _proj(x2d, gain, scale, shift, w_bf, rows_per_group, *, tm, tn, qk=None, silu_width=0, name):
    m, d = x2d.shape
    n = w_bf.shape[1]
    tiles_per_group = rows_per_group // tm
    in_specs = [
        pl.BlockSpec((tm, d), lambda i, j: (i, 0)),
        pl.BlockSpec((1, d), lambda i, j: (0, 0)),
        pl.BlockSpec((1, 1, d), lambda i, j: (i // tiles_per_group, 0, 0)),
        pl.BlockSpec((1, 1, d), lambda i, j: (i // tiles_per_group, 0, 0)),
        pl.BlockSpec((d, tn), lambda i, j: (0, j)),
    ]
    args = [x2d, gain.reshape(1, d), scale, shift, w_bf]
    scratch = [pltpu.VMEM((tm, d), BF16)]
    n_qk_tiles, rope, q_scale = 0, False, 1.0
    if qk is not None:
        qg, kg, gmat, cos, sa, sb, q_width, q_scale = qk
        n_qk_tiles = q_width // tn
        rope = cos is not None
        const = lambda i, j: (0, 0)
        in_specs += [pl.BlockSpec((1, HEAD_LANES), const), pl.BlockSpec((1, HEAD_LANES), const),
                     pl.BlockSpec((HEAD_LANES, HEAD_LANES), const)]
        args += [qg, kg, gmat]
        if rope:
            pos_tiles = cos.shape[0] // tm
            pos = lambda i, j: (i % pos_tiles, 0)
            in_specs += [pl.BlockSpec((tm, HEAD_LANES), pos)] * 3
            args += [cos, sa, sb]
        else:
            dummy = jnp.zeros((8, HEAD_LANES), F32)
            in_specs += [pl.BlockSpec((8, HEAD_LANES), const)] * 3
            args += [dummy, dummy, dummy]
        scratch.append(pltpu.VMEM((tm, tn), F32))
    kern = functools.partial(_proj_kernel, n_qk_tiles=n_qk_tiles, n_silu_tiles=silu_width // tn,
                             rope=rope, q_scale=q_scale, row_chunk=64, ep_rows=256)
    return pl.pallas_call(
        kern,
        grid=(m // tm, n // tn),
        in_specs=in_specs,
        out_specs=pl.BlockSpec((tm, tn), lambda i, j: (i, j)),
        out_shape=jax.ShapeDtypeStruct((m, n), BF16),
        scratch_shapes=scratch,
        compiler_params=_params("arbitrary", "arbitrary"),
        name=name,
    )(*args)


def _stage_keys_values(kc_ref, vc_ref, kl_ref, vl_ref, kb_ref, vt_ref, n_ctx_chunks, n_lat_chunks):
    def transposed(v):
        return v.astype(F32).T.astype(BF16)

    for c in range(n_ctx_chunks):
        rows = slice(c * KEY_CHUNK, (c + 1) * KEY_CHUNK)
        kb_ref[c] = kc_ref[0, rows, :]
        vt_ref[c] = transposed(vc_ref[0, rows, :])
    if n_lat_chunks:
        def body(c, carry):
            rows = pl.ds(pl.multiple_of(c * KEY_CHUNK, KEY_CHUNK), KEY_CHUNK)
            kb_ref[n_ctx_chunks + c] = kl_ref[0, rows, :]
            vt_ref[n_ctx_chunks + c] = transposed(vl_ref[0, rows, :])
            return carry
        lax.fori_loop(0, n_lat_chunks, body, 0)


def _attn_refs(refs, n_lat_chunks):
    if n_lat_chunks:
        return refs
    scal_ref, q_ref, kc_ref, vc_ref, za_ref, sg_ref, o_ref, kb_ref, vt_ref = refs
    return scal_ref, q_ref, kc_ref, vc_ref, None, None, za_ref, sg_ref, o_ref, kb_ref, vt_ref


def _query_maps(q):
    lane = lax.broadcasted_iota(jnp.int32, q.shape, 1)
    zero = jnp.zeros_like(q)
    return jnp.where(lane < HEAD_DIM, q, zero), jnp.where(lane >= HEAD_DIM, q, zero)


def _attn_finish(o_t, za, sg_ref, lam_init):
    o = o_t.T
    ms = jnp.mean(o * o, axis=-1, keepdims=True)
    y = o * lax.rsqrt(ms + EPS) * sg_ref[...] * (1.0 - lam_init)
    return (y * _silu(za.astype(F32))).astype(BF16)


def _attn_kernel(*refs, n_ctx_chunks, n_lat_chunks, lam_init):
    (scal_ref, q_ref, kc_ref, vc_ref, kl_ref, vl_ref, za_ref, sg_ref,
     o_ref, kb_ref, vt_ref) = _attn_refs(refs, n_lat_chunks)
    tq = q_ref.shape[1]
    n_chunks = n_ctx_chunks + n_lat_chunks

    @pl.when(pl.program_id(2) == 0)
    def _stage():
        _stage_keys_values(kc_ref, vc_ref, kl_ref, vl_ref, kb_ref, vt_ref, n_ctx_chunks, n_lat_chunks)

    q_maps = _query_maps(q_ref[0])

    def body(c, carry):
        kblk = kb_ref[c]
        vtb = vt_ref[c]
        out = []
        for mp in range(2):
            m_old, l_old, o_old = carry[3 * mp:3 * mp + 3]
            s = lax.dot_general(kblk, q_maps[mp], (((1,), (1,)), ((), ())),
                                preferred_element_type=F32)
            m_new = jnp.maximum(m_old, jnp.max(s, axis=0, keepdims=True))
            p = jnp.exp(s - m_new)
            alpha = jnp.exp(m_old - m_new)
            l_new = alpha * l_old + jnp.sum(p, axis=0, keepdims=True)
            o_new = alpha * o_old + jnp.dot(vtb, p.astype(BF16), preferred_element_type=F32)
            out += [m_new, l_new, o_new]
        return tuple(out)

    m0 = jnp.full((1, tq), -1e30, F32)
    l0 = jnp.zeros((1, tq), F32)
    o0 = jnp.zeros((HEAD_LANES, tq), F32)
    _, l_a, o_a, _, l_b, o_b = lax.fori_loop(0, n_chunks, body, (m0, l0, o0, m0, l0, o0))
    o_ref[0] = _attn_finish(o_a / l_a - scal_ref[0] * (o_b / l_b), za_ref[0], sg_ref, lam_init)


def _attn_bounded_kernel(*refs, n_ctx_chunks, n_lat_chunks, lam_init):
    (scal_ref, q_ref, kc_ref, vc_ref, kl_ref, vl_ref, za_ref, sg_ref,
     o_ref, kb_ref, vt_ref) = _attn_refs(refs, n_lat_chunks)
    tq = q_ref.shape[1]
    n_chunks = n_ctx_chunks + n_lat_chunks

    @pl.when(pl.program_id(2) == 0)
    def _stage():
        _stage_keys_values(kc_ref, vc_ref, kl_ref, vl_ref, kb_ref, vt_ref, n_ctx_chunks, n_lat_chunks)

    shift = scal_ref[1]
    steps = [(t, c) for t in range(tq // Q_SUB) for c in range(n_chunks)]
    q_maps = {}

    def scores(t, c):
        if t not in q_maps:
            q_maps[t] = _query_maps(q_ref[0, t * Q_SUB:(t + 1) * Q_SUB, :])
        kblk = kb_ref[c]
        return [lax.dot_general(kblk, q_maps[t][mp], (((1,), (1,)), ((), ())),
                                preferred_element_type=F32) for mp in range(2)]

    s_next = scores(*steps[0])
    for i, (t, c) in enumerate(steps):
        if c == 0:
            l_acc = [jnp.zeros((SUBLANES, Q_SUB), F32) for _ in range(2)]
            o_acc = [jnp.zeros((HEAD_LANES, Q_SUB), F32) for _ in range(2)]
        s_cur = s_next
        if i + 1 < len(steps):
            s_next = scores(*steps[i + 1])
        vtb = vt_ref[c]
        for mp in range(2):
            p = jnp.exp(s_cur[mp] - shift)
            l_acc[mp] = l_acc[mp] + jnp.sum(p.reshape(KEY_CHUNK // SUBLANES, SUBLANES, Q_SUB), axis=0)
            o_acc[mp] = o_acc[mp] + jnp.dot(vtb, p.astype(BF16), preferred_element_type=F32)
        if c == n_chunks - 1:
            l_a = jnp.sum(l_acc[0], axis=0, keepdims=True)
            l_b = jnp.sum(l_acc[1], axis=0, keepdims=True)
            rows = slice(t * Q_SUB, (t + 1) * Q_SUB)
            o_ref[0, rows, :] = _attn_finish(o_acc[0] / l_a - scal_ref[0] * (o_acc[1] / l_b),
                                             za_ref[0, rows, :], sg_ref, lam_init)


def _attention(scal, proj_q, proj_ctx, proj_lat, subln_g, *, tq, att_width, lam_init, bounded, name):
    bsz, lq, _ = proj_q.shape
    n_ctx = proj_ctx.shape[1]
    nh = att_width // HEAD_LANES
    n_ctx_chunks = n_ctx // KEY_CHUNK
    n_lat_chunks = 0 if proj_lat is None else proj_lat.shape[1] // KEY_CHUNK
    k_off, v_off, z_off = nh, 2 * nh, 3 * nh
    in_specs = [
        pl.BlockSpec(memory_space=pltpu.SMEM),
        pl.BlockSpec((1, tq, HEAD_LANES), lambda b, h, i: (b, i, h)),
        pl.BlockSpec((1, n_ctx, HEAD_LANES), lambda b, h, i: (b, 0, k_off + h)),
        pl.BlockSpec((1, n_ctx, HEAD_LANES), lambda b, h, i: (b, 0, v_off + h)),
    ]
    args = [scal, proj_q, proj_ctx, proj_ctx]
    if n_lat_chunks:
        n_lat = proj_lat.shape[1]
        in_specs += [pl.BlockSpec((1, n_lat, HEAD_LANES), lambda b, h, i: (b, 0, k_off + h)),
                     pl.BlockSpec((1, n_lat, HEAD_LANES), lambda b, h, i: (b, 0, v_off + h))]
        args += [proj_lat, proj_lat]
    in_specs += [pl.BlockSpec((1, tq, HEAD_LANES), lambda b, h, i: (b, i, z_off + h)),
                 pl.BlockSpec((1, HEAD_LANES), lambda b, h, i: (0, 0))]
    args += [proj_q, subln_g.reshape(1, HEAD_LANES)]
    n_chunks = n_ctx_chunks + n_lat_chunks
    kern = functools.partial(_attn_bounded_kernel if bounded else _attn_kernel,
                             n_ctx_chunks=n_ctx_chunks, n_lat_chunks=n_lat_chunks, lam_init=lam_init)
    return pl.pallas_call(
        kern,
        grid=(bsz, nh, lq // tq),
        in_specs=in_specs,
        out_specs=pl.BlockSpec((1, tq, HEAD_LANES), lambda b, h, i: (b, i, h)),
        out_shape=jax.ShapeDtypeStruct((bsz, lq, att_width), BF16),
        scratch_shapes=[pltpu.VMEM((n_chunks, KEY_CHUNK, HEAD_LANES), BF16),
                        pltpu.VMEM((n_chunks, HEAD_LANES, KEY_CHUNK), BF16)],
        compiler_params=_params("arbitrary", "arbitrary", "arbitrary"),
        name=name,
    )(*args)


def _conv_kernel(gv_ref, gg_ref, zb_ref, gvp_ref, ggp_ref, gvn_ref, ggn_ref,
                 w_ref, b_ref, lg_ref, lb_ref, o_ref, ypad_ref, ysh_ref, *, row_chunk):
    i = pl.program_id(1)
    n_tiles = pl.num_programs(1)
    tl = o_ref.shape[1]
    padded = tl + 2 * CONV_HALO

    def glu(val_ref, gate_ref, rows):
        return val_ref[0, rows, :].astype(F32) * jax.nn.sigmoid(gate_ref[0, rows, :].astype(F32))

    halo = slice(0, CONV_HALO)
    ypad_ref[0:CONV_HALO, :] = jnp.where(i > 0, glu(gvp_ref, ggp_ref, halo), 0.0)
    ypad_ref[CONV_HALO + tl:padded, :] = jnp.where(i < n_tiles - 1, glu(gvn_ref, ggn_ref, halo), 0.0)
    for r in range(tl // row_chunk):
        rows = slice(r * row_chunk, (r + 1) * row_chunk)
        ypad_ref[CONV_HALO + r * row_chunk:CONV_HALO + (r + 1) * row_chunk, :] = glu(gv_ref, gg_ref, rows)

    span = padded - SUBLANES
    for s in range(1, SUBLANES):
        for r in range(0, span, row_chunk):
            n = min(row_chunk, span - r)
            ysh_ref[s, r:r + n, :] = ypad_ref[r + s:r + s + n, :]

    first = CONV_HALO - CONV_K // 2
    for r in range(tl // row_chunk):
        acc = None
        for t in range(CONV_K):
            start = r * row_chunk + first + t
            phase, base = start % SUBLANES, start - start % SUBLANES
            src = ypad_ref[base:base + row_chunk, :] if phase == 0 else ysh_ref[phase, base:base + row_chunk, :]
            term = src.reshape(row_chunk // SUBLANES, SUBLANES, -1) * w_ref[t][None]
            acc = term if acc is None else acc + term
        acc = acc.reshape(row_chunk, -1) + b_ref[...]
        mu = jnp.mean(acc, axis=-1, keepdims=True)
        xc = acc - mu
        var = jnp.mean(xc * xc, axis=-1, keepdims=True)
        y = xc * lax.rsqrt(var + EPS) * lg_ref[...] + lb_ref[...]
        rows = slice(r * row_chunk, (r + 1) * row_chunk)
        o_ref[0, rows, :] = (_silu(y) * _silu(zb_ref[0, rows, :].astype(F32))).astype(BF16)


def _conv_branch(proj, conv_w, conv_b, cln_g, cln_b, *, tl, col0, width, name):
    bsz, length, _ = proj.shape
    n_tiles = length // tl
    halo_per_tile = tl // CONV_HALO
    last_halo = length // CONV_HALO - 1
    main = lambda off: pl.BlockSpec((1, tl, width), lambda b, i: (b, i, col0 + off))
    prev = lambda off: pl.BlockSpec(
        (1, CONV_HALO, width), lambda b, i: (b, jnp.maximum(i * halo_per_tile - 1, 0), col0 + off))
    nxt = lambda off: pl.BlockSpec(
        (1, CONV_HALO, width),
        lambda b, i: (b, jnp.minimum((i + 1) * halo_per_tile, last_halo), col0 + off))
    row = lambda k: pl.BlockSpec((k, width), lambda b, i: (0, 0))
    padded = tl + 2 * CONV_HALO
    return pl.pallas_call(
        functools.partial(_conv_kernel, row_chunk=32),
        grid=(bsz, n_tiles),
        in_specs=[main(0), main(1), main(2), prev(0), prev(1), nxt(0), nxt(1),
                  pl.BlockSpec((CONV_K, SUBLANES, width), lambda b, i: (0, 0, 0)),
                  row(1), row(1), row(1)],
        out_specs=pl.BlockSpec((1, tl, width), lambda b, i: (b, i, 0)),
        out_shape=jax.ShapeDtypeStruct((bsz, length, width), BF16),
        scratch_shapes=[pltpu.VMEM((padded, width), F32),
                        pltpu.VMEM((SUBLANES, padded, width), F32)],
        compiler_params=_params("arbitrary", "arbitrary"),
        name=name,
    )(proj, proj, proj, proj, proj, proj, proj,
      jnp.broadcast_to(conv_w[:, None, :], (CONV_K, SUBLANES, width)),
      conv_b.reshape(1, width), cln_g.reshape(1, width), cln_b.reshape(1, width))


def _out_ab_kernel(a_ref, c_ref, wa_ref, wc_ref, x_ref, gate_ref, o_ref):
    y = (jnp.dot(a_ref[...], wa_ref[...], preferred_element_type=F32)
         + jnp.dot(c_ref[...], wc_ref[...], preferred_element_type=F32))
    o_ref[...] = x_ref[...] + gate_ref[0] * y


def _out_ab(att, conv, w_bf, x2d, gate, rows_per_group, *, tm, tn, name):
    m, d = x2d.shape
    ka = att.shape[1]
    kc = conv.shape[1]
    tiles_per_group = rows_per_group // tm
    return pl.pallas_call(
        _out_ab_kernel,
        grid=(m // tm, d // tn),
        in_specs=[pl.BlockSpec((tm, ka), lambda i, j: (i, 0)),
                  pl.BlockSpec((tm, kc), lambda i, j: (i, 0)),
                  pl.BlockSpec((ka, tn), lambda i, j: (0, j)),
                  pl.BlockSpec((kc, tn), lambda i, j: (ka // kc, j)),
                  pl.BlockSpec((tm, tn), lambda i, j: (i, j)),
                  pl.BlockSpec((1, 1, tn), lambda i, j: (i // tiles_per_group, 0, j))],
        out_specs=pl.BlockSpec((tm, tn), lambda i, j: (i, j)),
        out_shape=jax.ShapeDtypeStruct((m, d), F32),
        compiler_params=_params("arbitrary", "arbitrary"),
        name=name,
    )(att, conv, w_bf, w_bf, x2d, gate)


def _out_c_kernel(of_ref, ob_ref, z_ref, g_ref, w_ref, x_ref, gate_ref, o_ref, y_ref, *, row_chunk):
    j = pl.program_id(1)
    tm, d = y_ref.shape

    @pl.when(j == 0)
    def _readout():
        def body(r, carry):
            rows = pl.ds(pl.multiple_of(r * row_chunk, row_chunk), row_chunk)
            for h in range(d // HEAD_LANES):
                cols = slice(h * HEAD_LANES, (h + 1) * HEAD_LANES)
                o = of_ref[rows, cols].astype(F32) + ob_ref[rows, cols].astype(F32)
                ms = jnp.mean(o * o, axis=-1, keepdims=True)
                y = o * lax.rsqrt(ms + EPS) * g_ref[...]
                y_ref[rows, cols] = (y * _silu(z_ref[rows, cols].astype(F32))).astype(BF16)
            return carry
        lax.fori_loop(0, tm // row_chunk, body, 0)

    y = jnp.dot(y_ref[...], w_ref[...], preferred_element_type=F32)
    o_ref[...] = x_ref[...] + gate_ref[0] * y


def _out_c(o_f, o_b, proj, z_col, onorm_g, w_bf, x2d, gate, rows_per_group, *, tm, tn, name):
    m, d = x2d.shape
    tiles_per_group = rows_per_group // tm
    return pl.pallas_call(
        functools.partial(_out_c_kernel, row_chunk=64),
        grid=(m // tm, d // tn),
        in_specs=[pl.BlockSpec((tm, d), lambda i, j: (i, 0)),
                  pl.BlockSpec((tm, d), lambda i, j: (i, 0)),
                  pl.BlockSpec((tm, d), lambda i, j: (i, z_col)),
                  pl.BlockSpec((1, HEAD_LANES), lambda i, j: (0, 0)),
                  pl.BlockSpec((d, tn), lambda i, j: (0, j)),
                  pl.BlockSpec((tm, tn), lambda i, j: (i, j)),
                  pl.BlockSpec((1, 1, tn), lambda i, j: (i // tiles_per_group, 0, j))],
        out_specs=pl.BlockSpec((tm, tn), lambda i, j: (i, j)),
        out_shape=jax.ShapeDtypeStruct((m, d), F32),
        scratch_shapes=[pltpu.VMEM((tm, d), BF16)],
        compiler_params=_params("arbitrary", "arbitrary"),
        name=name,
    )(o_f, o_b, proj, onorm_g.reshape(1, HEAD_LANES), w_bf, x2d, gate)


def _chunk_running_sum(x, tri_bf):
    hi = x.astype(BF16)
    lo = (x - hi.astype(F32)).astype(BF16)
    return (jnp.dot(tri_bf, hi, preferred_element_type=F32)
            + jnp.dot(tri_bf, lo, preferred_element_type=F32))


def _scan_kernel(*refs, has_s0, emit_o, emit_s):
    refs = list(refs)
    qf_ref, vf_ref, uf_ref, qb_ref, vb_ref, ub_ref, lb_ref, tri_ref = refs[:8]
    refs = refs[8:]
    s0_ref = refs.pop(0) if has_s0 else None
    of_ref, ob_ref = (refs.pop(0), refs.pop(0)) if emit_o else (None, None)
    sout_ref = refs.pop(0) if emit_s else None
    st_ref, = refs
    n = pl.program_id(2)
    tb = qf_ref.shape[1]

    @pl.when(n == 0)
    def _init():
        if has_s0:
            st_ref[...] = s0_ref[:, 0, 0]
        else:
            st_ref[...] = jnp.zeros_like(st_ref)

    sources = ((qf_ref, vf_ref, uf_ref, of_ref), (qb_ref, vb_ref, ub_ref, ob_ref))
    blk = tri_ref.shape[1]
    n_sub = tb // blk
    n_chunks = blk // HGRN_CHUNK
    work = []
    for d in range(2):
        for sb in (range(n_sub - 1, -1, -1) if d == 1 else range(n_sub)):
            work.append(dict(d=d, rows=slice(sb * blk, (sb + 1) * blk), base=sb * blk))

    for w in work:
        d = w["d"]
        lbd = lb_ref[d]
        sig = jax.nn.sigmoid(sources[d][2][0, w["rows"], :].astype(F32))
        logf = jnp.log(lbd + (1.0 - lbd) * sig)
        w["k"] = (1.0 - lbd) * (1.0 - sig)
        w["b"] = _chunk_running_sum(logf, tri_ref[d].astype(BF16))

    for w in work:
        d = w["d"]
        q_ref, v_ref = sources[d][:2]
        decay = jnp.exp(w["b"])
        k_inv = w["k"] / decay
        v_bf = v_ref[0, w["rows"], :]
        q_dec = (q_ref[0, w["rows"], :].astype(F32) * decay).astype(BF16)
        if emit_o:
            w["a"] = lax.dot_general(q_dec, k_inv.astype(BF16), (((1,), (1,)), ((), ())),
                                     preferred_element_type=F32)
        chunks = []
        for c in (range(n_chunks - 1, -1, -1) if d == 1 else range(n_chunks)):
            rows = slice(c * HGRN_CHUNK, (c + 1) * HGRN_CHUNK)
            last = c * HGRN_CHUNK if d == 1 else (c + 1) * HGRN_CHUNK - 1
            decay_end = decay[last:last + 1, :]
            k_end = (k_inv[rows] * decay_end).astype(BF16)
            kv = lax.dot_general(v_bf[rows], k_end, (((0,), (0,)), ((), ())),
                                 preferred_element_type=F32)
            chunks.append((rows, decay_end, kv))
        w.update(q_dec=q_dec, v_bf=v_bf, chunks=chunks)

    if emit_o:
        for w in work:
            a = jnp.where(tri_ref[w["d"]] > 0.5, w["a"], 0.0).astype(BF16)
            w["o_intra"] = jnp.dot(a, w["v_bf"], preferred_element_type=F32)

    for d in range(2):
        o_ref = sources[d][3]
        st = st_ref[d]
        for w in work:
            if w["d"] != d:
                continue
            for rows, decay_end, kv in w["chunks"]:
                if emit_o:
                    o = w["o_intra"][rows] + lax.dot_general(w["q_dec"][rows], st.astype(BF16),
                                                             (((1,), (1,)), ((), ())),
                                                             preferred_element_type=F32)
                    o_ref[0, w["base"] + rows.start:w["base"] + rows.stop, :] = o.astype(BF16)
                st = decay_end * st + kv
        st_ref[d] = st

    if emit_s:
        @pl.when(n == pl.num_programs(2) - 1)
        def _final():
            sout_ref[:, 0, 0] = st_ref[...]


def _scan(proj, lb, s0, *, tb, n_heads, emit_o, emit_s, name):
    bsz, length, _ = proj.shape
    nb = length // tb
    hd = HEAD_LANES
    fwd = lambda off: pl.BlockSpec((1, tb, hd), lambda b, h, n: (b, n, off * n_heads + h))
    bwd = lambda off: pl.BlockSpec((1, tb, hd), lambda b, h, n: (b, nb - 1 - n, off * n_heads + h))
    state = pl.BlockSpec((2, 1, 1, hd, hd), lambda b, h, n: (0, b, h, 0, 0))
    blk = min(tb, SCAN_BLOCK)
    tok = jnp.arange(blk)
    same_chunk = (tok[:, None] // HGRN_CHUNK) == (tok[None, :] // HGRN_CHUNK)
    tri = jnp.stack([same_chunk & (tok[None, :] <= tok[:, None]),
                     same_chunk & (tok[None, :] >= tok[:, None])]).astype(F32)
    in_specs = [fwd(0), fwd(1), fwd(2), bwd(0), bwd(1), bwd(3),
                pl.BlockSpec((2, 1, hd), lambda b, h, n: (0, 0, h)),
                pl.BlockSpec((2, blk, blk), lambda b, h, n: (0, 0, 0))]
    args = [proj] * 6 + [lb, tri]
    if s0 is not None:
        in_specs.append(state)
        args.append(s0)
    out_specs, out_shape = [], []
    if emit_o:
        out_specs += [pl.BlockSpec((1, tb, hd), lambda b, h, n: (b, n, h)),
                      pl.BlockSpec((1, tb, hd), lambda b, h, n: (b, nb - 1 - n, h))]
        out_shape += [jax.ShapeDtypeStruct((bsz, length, n_heads * hd), BF16)] * 2
    if emit_s:
        out_specs.append(state)
        out_shape.append(jax.ShapeDtypeStruct((2, bsz, n_heads, hd, hd), F32))
    kern = functools.partial(_scan_kernel, has_s0=s0 is not None, emit_o=emit_o, emit_s=emit_s)
    return pl.pallas_call(
        kern,
        grid=(bsz, n_heads, nb),
        in_specs=in_specs,
        out_specs=out_specs,
        out_shape=out_shape,
        scratch_shapes=[pltpu.VMEM((2, hd, hd), F32)],
        compiler_params=_params("arbitrary", "arbitrary", "arbitrary"),
        name=name,
    )(*args)


def _rope_tables(length):
    half = HEAD_DIM // 2
    rows = length // GRID_W
    row = jnp.repeat(jnp.arange(rows), GRID_W).astype(F32)
    col = jnp.tile(jnp.arange(GRID_W), rows).astype(F32)
    inv = ROPE_THETA ** (-jnp.arange(0, half, 2, dtype=F32) / half)
    def axis_angles(pos):
        a = pos[:, None] * inv[None, :]
        return jnp.concatenate([a, a], axis=-1)
    ang = jnp.concatenate([axis_angles(row), axis_angles(col)], axis=-1)
    ang = jnp.concatenate([ang, ang], axis=-1)
    first_half = (jnp.arange(HEAD_LANES) % half) < half // 2
    sin = jnp.sin(ang)
    return jnp.cos(ang), jnp.where(first_half, -sin, 0.0), jnp.where(first_half, 0.0, sin)


def kernel(x, c, ctx, c_ctx, w_ada, b_ada, norm_g, w_in_ab, w_out_ab, qn_g, kn_g, lam_q1, lam_k1,
           lam_q2, lam_k2, subln_g, conv_w, conv_b, cln_g, cln_b, w_in_c, w_out_c, lb_gamma, onorm_g):
    bsz, seq, d = x.shape
    n_ctx = ctx.shape[1]
    depth = w_ada.shape[0]
    assert depth == 2 and w_in_ab.shape[0] == 1 and w_in_c.shape[0] == 1
    att_width = N_HEADS * HEAD_LANES
    conv_width = conv_w.shape[-1]
    assert w_in_ab.shape[-1] == 4 * att_width + 3 * conv_width and conv_width == att_width

    cond_rows = jnp.concatenate([c, c_ctx[None, :], jnp.zeros((8 - bsz - 1, d), F32)], axis=0)
    mod = _ada(cond_rows, w_ada, b_ada)
    def mods(l, rows):
        m = mod[l, rows][:, None, :]
        return m[..., :d], m[..., d:2 * d], m[..., 2 * d:]

    cos, sin_a, sin_b = _rope_tables(seq)
    gmat = jnp.kron(jnp.eye(2, dtype=F32), jnp.full((HEAD_DIM, HEAD_DIM), 1.0 / HEAD_DIM, F32)
                    ).astype(BF16)
    two = lambda g: jnp.concatenate([g, g]).reshape(1, HEAD_LANES)
    q_scale = HEAD_DIM ** -0.5

    x2d = x.reshape(bsz * seq, d)
    ctx2d = ctx.reshape(bsz * n_ctx, d)
    lat_rows = slice(0, bsz)
    ctx_rows = slice(bsz, bsz + 1)

    lam_init = 0.8 - 0.6 * math.exp(-0.3 * 0)
    lam = (jnp.exp(jnp.sum(lam_q1[0] * lam_k1[0])) - jnp.exp(jnp.sum(lam_q2[0] * lam_k2[0]))
           + lam_init).reshape(1)
    shift, scale, gate = mods(0, lat_rows)
    shift_c, scale_c, gate_c = mods(0, ctx_rows)
    w_in = w_in_ab[0].astype(BF16)
    w_out = w_out_ab[0].astype(BF16)
    ab_in = w_in.shape[1]
    qk_lat = (two(qn_g[0]), two(kn_g[0]), gmat, cos, sin_a, sin_b, att_width, q_scale)
    qk_ctx = (two(qn_g[0]), two(kn_g[0]), gmat, None, None, None, att_width, q_scale)
    p_lat = _proj(x2d, norm_g[0], scale, shift, w_in, seq, tm=1024, tn=512, qk=qk_lat,
                  name="proj_ab_lat").reshape(bsz, seq, ab_in)
    p_ctx = _proj(ctx2d, norm_g[0], scale_c, shift_c, w_in, bsz * n_ctx, tm=1024, tn=512, qk=qk_ctx,
                  name="proj_ab_ctx").reshape(bsz, n_ctx, ab_in)
    bound = math.sqrt(HEAD_DIM) * jnp.max(jnp.abs(qn_g[0])) * jnp.max(jnp.abs(kn_g[0]))
    scal = jnp.concatenate([lam, bound.reshape(1)])
    def attend(bounded):
        def run():
            lat = _attention(scal, p_lat, p_ctx, p_lat, subln_g[0], tq=512, att_width=att_width,
                             lam_init=lam_init, bounded=bounded,
                             name="attn_lat_bounded" if bounded else "attn_lat")
            ctx_ = _attention(scal, p_ctx, p_ctx, None, subln_g[0], tq=n_ctx, att_width=att_width,
                              lam_init=lam_init, bounded=bounded,
                              name="attn_ctx_bounded" if bounded else "attn_ctx")
            return lat, ctx_
        return run
    att_lat, att_ctx = lax.cond(bound <= MAX_FIXED_SHIFT, attend(True), attend(False))
    conv_col0 = 4 * att_width // conv_width
    conv_lat = _conv_branch(p_lat, conv_w[0], conv_b[0], cln_g[0], cln_b[0], tl=256,
                            col0=conv_col0, width=conv_width, name="conv_lat")
    conv_ctx = _conv_branch(p_ctx, conv_w[0], conv_b[0], cln_g[0], cln_b[0], tl=256,
                            col0=conv_col0, width=conv_width, name="conv_ctx")
    x2d = _out_ab(att_lat.reshape(bsz * seq, att_width), conv_lat.reshape(bsz * seq, conv_width),
                  w_out, x2d, gate, seq, tm=1024, tn=1024, name="out_ab_lat")
    ctx2d = _out_ab(att_ctx.reshape(bsz * n_ctx, att_width),
                    conv_ctx.reshape(bsz * n_ctx, conv_width),
                    w_out, ctx2d, gate_c, bsz * n_ctx, tm=512, tn=1024, name="out_ab_ctx")

    p = jax.nn.softmax(lb_gamma.astype(F32), axis=1)
    lb_all = jnp.cumsum(p, axis=1) - p[:, :1]
    lb = lb_all[:, 1].reshape(2, 1, d)
    shift, scale, gate = mods(1, lat_rows)
    shift_c, scale_c, _ = mods(1, ctx_rows)
    w_in = w_in_c[0].astype(BF16)
    w_out = w_out_c[0].astype(BF16)
    c_in = w_in.shape[1]
    p_lat = _proj(x2d, norm_g[1], scale, shift, w_in, seq, tm=1024, tn=512, silu_width=d,
                  name="proj_c_lat").reshape(bsz, seq, c_in)
    p_ctx = _proj(ctx2d, norm_g[1], scale_c, shift_c, w_in, bsz * n_ctx, tm=1024, tn=512,
                  silu_width=d, name="proj_c_ctx").reshape(bsz, n_ctx, c_in)
    s_ctx, = _scan(p_ctx, lb, None, tb=n_ctx, n_heads=HGRN_HEADS, emit_o=False, emit_s=True,
                   name="scan_ctx")
    o_f, o_b = _scan(p_lat, lb, s_ctx, tb=512, n_heads=HGRN_HEADS, emit_o=True, emit_s=False,
                     name="scan_lat")
    out = _out_c(o_f.reshape(bsz * seq, d), o_b.reshape(bsz * seq, d),
                 p_lat.reshape(bsz * seq, c_in), 4, onorm_g[0], w_out, x2d, gate, seq,
                 tm=1024, tn=512, name="out_c_lat")
    return out.reshape(bsz, seq, d)
```

```python
import functools
import math

import jax
import jax.numpy as jnp
from jax import lax
from jax.experimental import pallas as pl
from jax.experimental.pallas import tpu as pltpu

F32 = jnp.float32
BF16 = jnp.bfloat16

EPS = 1e-6
GRID_W = 64
ROPE_THETA = 10000.0
HEAD_DIM = 64
HEAD_LANES = 128
N_HEADS = 8
CONV_K = 31
CONV_HALO = 16
SUBLANES = 8
HGRN_HEADS = 16
HGRN_CHUNK = 64
SCAN_BLOCK = 256

KEY_CHUNK = 256
Q_SUB = 256
MAX_FIXED_SHIFT = 40.0
VMEM_LIMIT = 48 * 1024 * 1024


def _silu(x):
    return x * jax.nn.sigmoid(x)


def _params(*sem):
    return pltpu.CompilerParams(dimension_semantics=sem, vmem_limit_bytes=VMEM_LIMIT)


def _ada_kernel(c_ref, w_ref, b_ref, o_ref):
    sc = _silu(c_ref[...])
    o_ref[0] = jnp.dot(sc, w_ref[0], precision=lax.Precision.HIGHEST,
                       preferred_element_type=F32) + b_ref[0]


def _ada(cond_rows, w_ada, b_ada):
    depth, d, n = w_ada.shape
    tn = 768
    return pl.pallas_call(
        _ada_kernel,
        grid=(depth, n // tn),
        in_specs=[pl.BlockSpec((8, d), lambda l, j: (0, 0)),
                  pl.BlockSpec((1, d, tn), lambda l, j: (l, 0, j)),
                  pl.BlockSpec((1, 1, tn), lambda l, j: (l, 0, j))],
        out_specs=pl.BlockSpec((1, 8, tn), lambda l, j: (l, 0, j)),
        out_shape=jax.ShapeDtypeStruct((depth, 8, n), F32),
        compiler_params=_params("arbitrary", "arbitrary"),
        name="ada",
    )(cond_rows, w_ada, b_ada.reshape(depth, 1, n))


def _group_mean_sq(x, gmat):
    sq = x * x
    hi = sq.astype(BF16)
    lo = (sq - hi.astype(F32)).astype(BF16)
    return (jnp.dot(hi, gmat, preferred_element_type=F32)
            + jnp.dot(lo, gmat, preferred_element_type=F32))


def _proj_kernel(*refs, n_qk_tiles, silu_tiles, rope, q_scale, row_chunk, ep_rows):
    if n_qk_tiles:
        (x_ref, g_ref, sc_ref, sh_ref, w_ref, qg_ref, kg_ref, gmat_ref,
         cos_ref, sa_ref, sb_ref, o_ref, xn_ref, acc_ref) = refs
    else:
        x_ref, g_ref, sc_ref, sh_ref, w_ref, o_ref, xn_ref = refs
    j = pl.program_id(1)
    tm, tn = o_ref.shape

    @pl.when(j == 0)
    def _normalise():
        gain = g_ref[...] * (1.0 + sc_ref[0])
        shift = sh_ref[0]

        def body(r, carry):
            r0 = pl.multiple_of(r * row_chunk, row_chunk)
            x = x_ref[pl.ds(r0, row_chunk), :]
            ms = jnp.mean(x * x, axis=-1, keepdims=True)
            xn_ref[pl.ds(r0, row_chunk), :] = (x * lax.rsqrt(ms + EPS) * gain + shift).astype(BF16)
            return carry

        lax.fori_loop(0, tm // row_chunk, body, 0, unroll=2)

    def matmul():
        return jnp.dot(xn_ref[...], w_ref[...], preferred_element_type=F32)

    def norm_rope(gain_ref, scale):
        acc_ref[...] = matmul()
        gmat = gmat_ref[...]
        gain = gain_ref[...]

        def body(r, carry):
            rows = pl.ds(pl.multiple_of(r * ep_rows, ep_rows), ep_rows)
            for c in range(tn // HEAD_LANES):
                cols = slice(c * HEAD_LANES, (c + 1) * HEAD_LANES)
                xs = acc_ref[rows, cols]
                y = xs * lax.rsqrt(_group_mean_sq(xs, gmat) + EPS) * gain
                if rope:
                    y = (y * cos_ref[rows, :]
                         + pltpu.roll(y, HEAD_LANES - 16, axis=1) * sa_ref[rows, :]
                         + pltpu.roll(y, 16, axis=1) * sb_ref[rows, :])
                if scale != 1.0:
                    y = y * scale
                o_ref[rows, cols] = y.astype(BF16)
            return carry

        lax.fori_loop(0, tm // ep_rows, body, 0)

    plain = j >= 2 * n_qk_tiles
    if n_qk_tiles:
        @pl.when(j < n_qk_tiles)
        def _q():
            norm_rope(qg_ref, q_scale)

        @pl.when((j >= n_qk_tiles) & (j < 2 * n_qk_tiles))
        def _k():
            norm_rope(kg_ref, 1.0)

    gated = functools.reduce(lambda a, b: a | b, [j == t for t in silu_tiles], j < 0)

    @pl.when(plain & gated)
    def _gated():
        o_ref[...] = _silu(matmul()).astype(BF16)

    @pl.when(plain & jnp.logical_not(gated))
    def _rest():
        o_ref[...] = matmul().astype(BF16)


def _proj(x2d, gain, scale, shift, w_bf, rows_per_group, *, tm, tn, qk=None, silu_cols=(), name):
    m, d = x2d.shape
    n = w_bf.shape[1]
    tiles_per_group = rows_per_group // tm
    in_specs = [
        pl.BlockSpec((tm, d), lambda i, j: (i, 0)),
        pl.BlockSpec((1, d), lambda i, j: (0, 0)),
        pl.BlockSpec((1, 1, d), lambda i, j: (i // tiles_per_group, 0, 0)),
        pl.BlockSpec((1, 1, d), lambda i, j: (i // tiles_per_group, 0, 0)),
        pl.BlockSpec((d, tn), lambda i, j: (0, j)),
    ]
    args = [x2d, gain.reshape(1, d), scale, shift, w_bf]
    scratch = [pltpu.VMEM((tm, d), BF16)]
    n_qk_tiles, rope, q_scale = 0, False, 1.0
    if qk is not None:
        qg, kg, gmat, cos, sa, sb, q_width, q_scale = qk
        n_qk_tiles = q_width // tn
        rope = cos is not None
        const = lambda i, j: (0, 0)
        in_specs += [pl.BlockSpec((1, HEAD_LANES), const), pl.BlockSpec((1, HEAD_LANES), const),
                     pl.BlockSpec((HEAD_LANES, HEAD_LANES), const)]
        args += [qg, kg, gmat]
        if rope:
            pos_tiles = cos.shape[0] // tm
            pos = lambda i, j: (i % pos_tiles, 0)
            in_specs += [pl.BlockSpec((tm, HEAD_LANES), pos)] * 3
            args += [cos, sa, sb]
        else:
            dummy = jnp.zeros((8, HEAD_LANES), F32)
            in_specs += [pl.BlockSpec((8, HEAD_LANES), const)] * 3
            args += [dummy, dummy, dummy]
        scratch.append(pltpu.VMEM((tm, tn), F32))
    silu_tiles = tuple(t for a, b in silu_cols for t in range(a // tn, b // tn))
    kern = functools.partial(_proj_kernel, n_qk_tiles=n_qk_tiles, silu_tiles=silu_tiles,
                             rope=rope, q_scale=q_scale, row_chunk=32, ep_rows=256)
    return pl.pallas_call(
        kern,
        grid=(m // tm, n // tn),
        in_specs=in_specs,
        out_specs=pl.BlockSpec((tm, tn), lambda i, j: (i, j)),
        out_shape=jax.ShapeDtypeStruct((m, n), BF16),
        scratch_shapes=scratch,
        compiler_params=_params("arbitrary", "arbitrary"),
        name=name,
    )(*args)


def _stage_keys_values(kc_ref, vc_ref, kl_ref, vl_ref, kb_ref, vt_ref, n_ctx_chunks, n_lat_chunks):
    def transposed(v):
        return v.astype(F32).T.astype(BF16)

    for c in range(n_ctx_chunks):
        rows = slice(c * KEY_CHUNK, (c + 1) * KEY_CHUNK)
        kb_ref[c] = kc_ref[0, rows, :]
        vt_ref[c] = transposed(vc_ref[0, rows, :])
    if n_lat_chunks:
        def body(c, carry):
            rows = pl.ds(pl.multiple_of(c * KEY_CHUNK, KEY_CHUNK), KEY_CHUNK)
            kb_ref[n_ctx_chunks + c] = kl_ref[0, rows, :]
            vt_ref[n_ctx_chunks + c] = transposed(vl_ref[0, rows, :])
            return carry
        lax.fori_loop(0, n_lat_chunks, body, 0)


def _attn_refs(refs, n_lat_chunks):
    if n_lat_chunks:
        return refs
    scal_ref, q_ref, kc_ref, vc_ref, za_ref, sg_ref, o_ref, kb_ref, vt_ref = refs
    return scal_ref, q_ref, kc_ref, vc_ref, None, None, za_ref, sg_ref, o_ref, kb_ref, vt_ref


def _query_maps(q):
    lane = lax.broadcasted_iota(jnp.int32, q.shape, 1)
    zero = jnp.zeros_like(q)
    return jnp.where(lane < HEAD_DIM, q, zero), jnp.where(lane >= HEAD_DIM, q, zero)


def _attn_finish(o_t, za, sg_ref, lam_init):
    o = o_t.T
    ms = jnp.mean(o * o, axis=-1, keepdims=True)
    y = o * lax.rsqrt(ms + EPS) * sg_ref[...] * (1.0 - lam_init)
    return (y * za.astype(F32)).astype(BF16)


def _attn_kernel(*refs, n_ctx_chunks, n_lat_chunks, lam_init):
    (scal_ref, q_ref, kc_ref, vc_ref, kl_ref, vl_ref, za_ref, sg_ref,
     o_ref, kb_ref, vt_ref) = _attn_refs(refs, n_lat_chunks)
    tq = q_ref.shape[1]
    n_chunks = n_ctx_chunks + n_lat_chunks

    @pl.when(pl.program_id(2) == 0)
    def _stage():
        _stage_keys_values(kc_ref, vc_ref, kl_ref, vl_ref, kb_ref, vt_ref, n_ctx_chunks, n_lat_chunks)

    q_maps = _query_maps(q_ref[0])

    def body(c, carry):
        kblk = kb_ref[c]
        vtb = vt_ref[c]
        out = []
        for mp in range(2):
            m_old, l_old, o_old = carry[3 * mp:3 * mp + 3]
            s = lax.dot_general(kblk, q_maps[mp], (((1,), (1,)), ((), ())),
                                preferred_element_type=F32)
            m_new = jnp.maximum(m_old, jnp.max(s, axis=0, keepdims=True))
            p = jnp.exp(s - m_new)
            alpha = jnp.exp(m_old - m_new)
            l_new = alpha * l_old + jnp.sum(p, axis=0, keepdims=True)
            o_new = alpha * o_old + jnp.dot(vtb, p.astype(BF16), preferred_element_type=F32)
            out += [m_new, l_new, o_new]
        return tuple(out)

    m0 = jnp.full((1, tq), -1e30, F32)
    l0 = jnp.zeros((1, tq), F32)
    o0 = jnp.zeros((HEAD_LANES, tq), F32)
    _, l_a, o_a, _, l_b, o_b = lax.fori_loop(0, n_chunks, body, (m0, l0, o0, m0, l0, o0))
    o_ref[0] = _attn_finish(o_a / l_a - scal_ref[0] * (o_b / l_b), za_ref[0], sg_ref, lam_init)


def _attn_bounded_kernel(*refs, n_ctx_chunks, n_lat_chunks, lam_init):
    (scal_ref, q_ref, kc_ref, vc_ref, kl_ref, vl_ref, za_ref, sg_ref,
     o_ref, kb_ref, vt_ref) = _attn_refs(refs, n_lat_chunks)
    tq = q_ref.shape[1]
    n_chunks = n_ctx_chunks + n_lat_chunks

    @pl.when(pl.program_id(2) == 0)
    def _stage():
        _stage_keys_values(kc_ref, vc_ref, kl_ref, vl_ref, kb_ref, vt_ref, n_ctx_chunks, n_lat_chunks)

    shift = scal_ref[1]
    steps = [(t, c) for t in range(tq // Q_SUB) for c in range(n_chunks)]
    q_maps = {}

    def scores(t, c):
        if t not in q_maps:
            q_maps[t] = _query_maps(q_ref[0, t * Q_SUB:(t + 1) * Q_SUB, :])
        kblk = kb_ref[c]
        return [lax.dot_general(kblk, q_maps[t][mp], (((1,), (1,)), ((), ())),
                                preferred_element_type=F32) for mp in range(2)]

    s_next = scores(*steps[0])
    for i, (t, c) in enumerate(steps):
        if c == 0:
            l_acc = [jnp.zeros((SUBLANES, Q_SUB), F32) for _ in range(2)]
            o_acc = [jnp.zeros((HEAD_LANES, Q_SUB), F32) for _ in range(2)]
        s_cur = s_next
        if i + 1 < len(steps):
            s_next = scores(*steps[i + 1])
        vtb = vt_ref[c]
        for mp in range(2):
            p = jnp.exp(s_cur[mp] - shift)
            l_acc[mp] = l_acc[mp] + jnp.sum(p.reshape(KEY_CHUNK // SUBLANES, SUBLANES, Q_SUB), axis=0)
            o_acc[mp] = o_acc[mp] + jnp.dot(vtb, p.astype(BF16), preferred_element_type=F32)
        if c == n_chunks - 1:
            l_a = jnp.sum(l_acc[0], axis=0, keepdims=True)
            l_b = jnp.sum(l_acc[1], axis=0, keepdims=True)
            rows = slice(t * Q_SUB, (t + 1) * Q_SUB)
            o_ref[0, rows, :] = _attn_finish(o_acc[0] / l_a - scal_ref[0] * (o_acc[1] / l_b),
                                             za_ref[0, rows, :], sg_ref, lam_init)


def _attention(scal, proj_q, proj_ctx, proj_lat, subln_g, *, tq, att_width, lam_init, bounded, name):
    bsz, lq, _ = proj_q.shape
    n_ctx = proj_ctx.shape[1]
    nh = att_width // HEAD_LANES
    n_ctx_chunks = n_ctx // KEY_CHUNK
    n_lat_chunks = 0 if proj_lat is None else proj_lat.shape[1] // KEY_CHUNK
    k_off, v_off, z_off = nh, 2 * nh, 3 * nh
    in_specs = [
        pl.BlockSpec(memory_space=pltpu.SMEM),
        pl.BlockSpec((1, tq, HEAD_LANES), lambda b, h, i: (b, i, h)),
        pl.BlockSpec((1, n_ctx, HEAD_LANES), lambda b, h, i: (b, 0, k_off + h)),
        pl.BlockSpec((1, n_ctx, HEAD_LANES), lambda b, h, i: (b, 0, v_off + h)),
    ]
    args = [scal, proj_q, proj_ctx, proj_ctx]
    if n_lat_chunks:
        n_lat = proj_lat.shape[1]
        in_specs += [pl.BlockSpec((1, n_lat, HEAD_LANES), lambda b, h, i: (b, 0, k_off + h)),
                     pl.BlockSpec((1, n_lat, HEAD_LANES), lambda b, h, i: (b, 0, v_off + h))]
        args += [proj_lat, proj_lat]
    in_specs += [pl.BlockSpec((1, tq, HEAD_LANES), lambda b, h, i: (b, i, z_off + h)),
                 pl.BlockSpec((1, HEAD_LANES), lambda b, h, i: (0, 0))]
    args += [proj_q, subln_g.reshape(1, HEAD_LANES)]
    n_chunks = n_ctx_chunks + n_lat_chunks
    kern = functools.partial(_attn_bounded_kernel if bounded else _attn_kernel,
                             n_ctx_chunks=n_ctx_chunks, n_lat_chunks=n_lat_chunks, lam_init=lam_init)
    return pl.pallas_call(
        kern,
        grid=(bsz, nh, lq // tq),
        in_specs=in_specs,
        out_specs=pl.BlockSpec((1, tq, HEAD_LANES), lambda b, h, i: (b, i, h)),
        out_shape=jax.ShapeDtypeStruct((bsz, lq, att_width), BF16),
        scratch_shapes=[pltpu.VMEM((n_chunks, KEY_CHUNK, HEAD_LANES), BF16),
                        pltpu.VMEM((n_chunks, HEAD_LANES, KEY_CHUNK), BF16)],
        compiler_params=_params("arbitrary", "arbitrary", "arbitrary"),
        name=name,
    )(*args)


def _conv_kernel(gv_ref, gg_ref, zb_ref, gvp_ref, ggp_ref, gvn_ref, ggn_ref,
                 w_ref, b_ref, lg_ref, lb_ref, o_ref, ypad_ref, ysh_ref, *, row_chunk):
    i = pl.program_id(1)
    n_tiles = pl.num_programs(1)
    tl = o_ref.shape[1]
    padded = tl + 2 * CONV_HALO

    def glu(val_ref, gate_ref, rows):
        return val_ref[0, rows, :].astype(F32) * jax.nn.sigmoid(gate_ref[0, rows, :].astype(F32))

    halo = slice(0, CONV_HALO)
    ypad_ref[0:CONV_HALO, :] = jnp.where(i > 0, glu(gvp_ref, ggp_ref, halo), 0.0)
    ypad_ref[CONV_HALO + tl:padded, :] = jnp.where(i < n_tiles - 1, glu(gvn_ref, ggn_ref, halo), 0.0)
    for r in range(tl // row_chunk):
        rows = slice(r * row_chunk, (r + 1) * row_chunk)
        ypad_ref[CONV_HALO + r * row_chunk:CONV_HALO + (r + 1) * row_chunk, :] = glu(gv_ref, gg_ref, rows)

    span = padded - SUBLANES
    for s in range(1, SUBLANES):
        for r in range(0, span, row_chunk):
            n = min(row_chunk, span - r)
            ysh_ref[s, r:r + n, :] = ypad_ref[r + s:r + s + n, :]

    first = CONV_HALO - CONV_K // 2
    for r in range(tl // row_chunk):
        acc = None
        for t in range(CONV_K):
            start = r * row_chunk + first + t
            phase, base = start % SUBLANES, start - start % SUBLANES
            src = ypad_ref[base:base + row_chunk, :] if phase == 0 else ysh_ref[phase, base:base + row_chunk, :]
            term = src.reshape(row_chunk // SUBLANES, SUBLANES, -1) * w_ref[t][None]
            acc = term if acc is None else acc + term
        acc = acc.reshape(row_chunk, -1) + b_ref[...]
        mu = jnp.mean(acc, axis=-1, keepdims=True)
        xc = acc - mu
        var = jnp.mean(xc * xc, axis=-1, keepdims=True)
        y = xc * lax.rsqrt(var + EPS) * lg_ref[...] + lb_ref[...]
        rows = slice(r * row_chunk, (r + 1) * row_chunk)
        o_ref[0, rows, :] = (_silu(y) * zb_ref[0, rows, :].astype(F32)).astype(BF16)


def _conv_branch(proj, conv_w, conv_b, cln_g, cln_b, *, tl, col0, width, name):
    bsz, length, _ = proj.shape
    n_tiles = length // tl
    halo_per_tile = tl // CONV_HALO
    last_halo = length // CONV_HALO - 1
    main = lambda off: pl.BlockSpec((1, tl, width), lambda b, i: (b, i, col0 + off))
    prev = lambda off: pl.BlockSpec(
        (1, CONV_HALO, width), lambda b, i: (b, jnp.maximum(i * halo_per_tile - 1, 0), col0 + off))
    nxt = lambda off: pl.BlockSpec(
        (1, CONV_HALO, width),
        lambda b, i: (b, jnp.minimum((i + 1) * halo_per_tile, last_halo), col0 + off))
    row = lambda k: pl.BlockSpec((k, width), lambda b, i: (0, 0))
    padded = tl + 2 * CONV_HALO
    return pl.pallas_call(
        functools.partial(_conv_kernel, row_chunk=32),
        grid=(bsz, n_tiles),
        in_specs=[main(0), main(1), main(2), prev(0), prev(1), nxt(0), nxt(1),
                  pl.BlockSpec((CONV_K, SUBLANES, width), lambda b, i: (0, 0, 0)),
                  row(1), row(1), row(1)],
        out_specs=pl.BlockSpec((1, tl, width), lambda b, i: (b, i, 0)),
        out_shape=jax.ShapeDtypeStruct((bsz, length, width), BF16),
        scratch_shapes=[pltpu.VMEM((padded, width), F32),
                        pltpu.VMEM((SUBLANES, padded, width), F32)],
        compiler_params=_params("arbitrary", "arbitrary"),
        name=name,
    )(proj, proj, proj, proj, proj, proj, proj,
      jnp.broadcast_to(conv_w[:, None, :], (CONV_K, SUBLANES, width)),
      conv_b.reshape(1, width), cln_g.reshape(1, width), cln_b.reshape(1, width))


def _out_ab_kernel(a_ref, c_ref, wa_ref, wc_ref, x_ref, gate_ref, o_ref):
    y = (jnp.dot(a_ref[...], wa_ref[...], preferred_element_type=F32)
         + jnp.dot(c_ref[...], wc_ref[...], preferred_element_type=F32))
    o_ref[...] = x_ref[...] + gate_ref[0] * y


def _out_ab(att, conv, w_bf, x2d, gate, rows_per_group, *, tm, tn, name):
    m, d = x2d.shape
    ka = att.shape[1]
    kc = conv.shape[1]
    tiles_per_group = rows_per_group // tm
    return pl.pallas_call(
        _out_ab_kernel,
        grid=(m // tm, d // tn),
        in_specs=[pl.BlockSpec((tm, ka), lambda i, j: (i, 0)),
                  pl.BlockSpec((tm, kc), lambda i, j: (i, 0)),
                  pl.BlockSpec((ka, tn), lambda i, j: (0, j)),
                  pl.BlockSpec((kc, tn), lambda i, j: (ka // kc, j)),
                  pl.BlockSpec((tm, tn), lambda i, j: (i, j)),
                  pl.BlockSpec((1, 1, tn), lambda i, j: (i // tiles_per_group, 0, j))],
        out_specs=pl.BlockSpec((tm, tn), lambda i, j: (i, j)),
        out_shape=jax.ShapeDtypeStruct((m, d), F32),
        compiler_params=_params("arbitrary", "arbitrary"),
        name=name,
    )(att, conv, w_bf, w_bf, x2d, gate)


def _out_c_kernel(of_ref, ob_ref, z_ref, g_ref, w_ref, x_ref, gate_ref, o_ref, y_ref, *, row_chunk):
    j = pl.program_id(1)
    tm, d = y_ref.shape

    @pl.when(j == 0)
    def _readout():
        def body(r, carry):
            rows = pl.ds(pl.multiple_of(r * row_chunk, row_chunk), row_chunk)
            for h in range(d // HEAD_LANES):
                cols = slice(h * HEAD_LANES, (h + 1) * HEAD_LANES)
                o = of_ref[rows, cols].astype(F32) + ob_ref[rows, cols].astype(F32)
                ms = jnp.mean(o * o, axis=-1, keepdims=True)
                y = o * lax.rsqrt(ms + EPS) * g_ref[...]
                y_ref[rows, cols] = (y * z_ref[rows, cols].astype(F32)).astype(BF16)
            return carry
        lax.fori_loop(0, tm // row_chunk, body, 0)

    y = jnp.dot(y_ref[...], w_ref[...], preferred_element_type=F32)
    o_ref[...] = x_ref[...] + gate_ref[0] * y


def _out_c(o_f, o_b, proj, z_col, onorm_g, w_bf, x2d, gate, rows_per_group, *, tm, tn, name):
    m, d = x2d.shape
    tiles_per_group = rows_per_group // tm
    return pl.pallas_call(
        functools.partial(_out_c_kernel, row_chunk=64),
        grid=(m // tm, d // tn),
        in_specs=[pl.BlockSpec((tm, d), lambda i, j: (i, 0)),
                  pl.BlockSpec((tm, d), lambda i, j: (i, 0)),
                  pl.BlockSpec((tm, d), lambda i, j: (i, z_col)),
                  pl.BlockSpec((1, HEAD_LANES), lambda i, j: (0, 0)),
                  pl.BlockSpec((d, tn), lambda i, j: (0, j)),
                  pl.BlockSpec((tm, tn), lambda i, j: (i, j)),
                  pl.BlockSpec((1, 1, tn), lambda i, j: (i // tiles_per_group, 0, j))],
        out_specs=pl.BlockSpec((tm, tn), lambda i, j: (i, j)),
        out_shape=jax.ShapeDtypeStruct((m, d), F32),
        scratch_shapes=[pltpu.VMEM((tm, d), BF16)],
        compiler_params=_params("arbitrary", "arbitrary"),
        name=name,
    )(o_f, o_b, proj, onorm_g.reshape(1, HEAD_LANES), w_bf, x2d, gate)


def _chunk_running_sum(x, tri_bf):
    hi = x.astype(BF16)
    lo = (x - hi.astype(F32)).astype(BF16)
    return (jnp.dot(tri_bf, hi, preferred_element_type=F32)
            + jnp.dot(tri_bf, lo, preferred_element_type=F32))


def _scan_kernel(*refs, has_s0, emit_o, emit_s):
    refs = list(refs)
    qf_ref, vf_ref, uf_ref, qb_ref, vb_ref, ub_ref, lb_ref, tri_ref = refs[:8]
    refs = refs[8:]
    s0_ref = refs.pop(0) if has_s0 else None
    of_ref, ob_ref = (refs.pop(0), refs.pop(0)) if emit_o else (None, None)
    sout_ref = refs.pop(0) if emit_s else None
    st_ref, = refs
    n = pl.program_id(2)
    tb = qf_ref.shape[1]

    @pl.when(n == 0)
    def _init():
        if has_s0:
            st_ref[...] = s0_ref[:, 0, 0]
        else:
            st_ref[...] = jnp.zeros_like(st_ref)

    sources = ((qf_ref, vf_ref, uf_ref, of_ref), (qb_ref, vb_ref, ub_ref, ob_ref))
    blk = tri_ref.shape[1]
    n_sub = tb // blk
    n_chunks = blk // HGRN_CHUNK
    work = []
    for d in range(2):
        for sb in (range(n_sub - 1, -1, -1) if d == 1 else range(n_sub)):
            work.append(dict(d=d, rows=slice(sb * blk, (sb + 1) * blk), base=sb * blk))

    for w in work:
        d = w["d"]
        lbd = lb_ref[d]
        sig = jax.nn.sigmoid(sources[d][2][0, w["rows"], :].astype(F32))
        logf = jnp.log(lbd + (1.0 - lbd) * sig)
        w["k"] = (1.0 - lbd) * (1.0 - sig)
        w["b"] = _chunk_running_sum(logf, tri_ref[d].astype(BF16))

    for w in work:
        d = w["d"]
        q_ref, v_ref = sources[d][:2]
        decay = jnp.exp(w["b"])
        k_inv = w["k"] / decay
        v_bf = v_ref[0, w["rows"], :]
        q_dec = (q_ref[0, w["rows"], :].astype(F32) * decay).astype(BF16)
        if emit_o:
            w["a"] = lax.dot_general(q_dec, k_inv.astype(BF16), (((1,), (1,)), ((), ())),
                                     preferred_element_type=F32)
        chunks = []
        for c in (range(n_chunks - 1, -1, -1) if d == 1 else range(n_chunks)):
            rows = slice(c * HGRN_CHUNK, (c + 1) * HGRN_CHUNK)
            last = c * HGRN_CHUNK if d == 1 else (c + 1) * HGRN_CHUNK - 1
            decay_end = decay[last:last + 1, :]
            k_end = (k_inv[rows] * decay_end).astype(BF16)
            kv = lax.dot_general(v_bf[rows], k_end, (((0,), (0,)), ((), ())),
                                 preferred_element_type=F32)
            chunks.append((rows, decay_end, kv))
        w.update(q_dec=q_dec, v_bf=v_bf, chunks=chunks)

    if emit_o:
        for w in work:
            a = jnp.where(tri_ref[w["d"]] > 0.5, w["a"], 0.0).astype(BF16)
            w["o_intra"] = jnp.dot(a, w["v_bf"], preferred_element_type=F32)

    for d in range(2):
        o_ref = sources[d][3]
        st = st_ref[d]
        for w in work:
            if w["d"] != d:
                continue
            for rows, decay_end, kv in w["chunks"]:
                if emit_o:
                    o = w["o_intra"][rows] + lax.dot_general(w["q_dec"][rows], st.astype(BF16),
                                                             (((1,), (1,)), ((), ())),
                                                             preferred_element_type=F32)
                    o_ref[0, w["base"] + rows.start:w["base"] + rows.stop, :] = o.astype(BF16)
                st = decay_end * st + kv
        st_ref[d] = st

    if emit_s:
        @pl.when(n == pl.num_programs(2) - 1)
        def _final():
            sout_ref[:, 0, 0] = st_ref[...]


def _scan(proj, lb, s0, *, tb, n_heads, emit_o, emit_s, name):
    bsz, length, _ = proj.shape
    nb = length // tb
    hd = HEAD_LANES
    fwd = lambda off: pl.BlockSpec((1, tb, hd), lambda b, h, n: (b, n, off * n_heads + h))
    bwd = lambda off: pl.BlockSpec((1, tb, hd), lambda b, h, n: (b, nb - 1 - n, off * n_heads + h))
    state = pl.BlockSpec((2, 1, 1, hd, hd), lambda b, h, n: (0, b, h, 0, 0))
    blk = min(tb, SCAN_BLOCK)
    tok = jnp.arange(blk)
    same_chunk = (tok[:, None] // HGRN_CHUNK) == (tok[None, :] // HGRN_CHUNK)
    tri = jnp.stack([same_chunk & (tok[None, :] <= tok[:, None]),
                     same_chunk & (tok[None, :] >= tok[:, None])]).astype(F32)
    in_specs = [fwd(0), fwd(1), fwd(2), bwd(0), bwd(1), bwd(3),
                pl.BlockSpec((2, 1, hd), lambda b, h, n: (0, 0, h)),
                pl.BlockSpec((2, blk, blk), lambda b, h, n: (0, 0, 0))]
    args = [proj] * 6 + [lb, tri]
    if s0 is not None:
        in_specs.append(state)
        args.append(s0)
    out_specs, out_shape = [], []
    if emit_o:
        out_specs += [pl.BlockSpec((1, tb, hd), lambda b, h, n: (b, n, h)),
                      pl.BlockSpec((1, tb, hd), lambda b, h, n: (b, nb - 1 - n, h))]
        out_shape += [jax.ShapeDtypeStruct((bsz, length, n_heads * hd), BF16)] * 2
    if emit_s:
        out_specs.append(state)
        out_shape.append(jax.ShapeDtypeStruct((2, bsz, n_heads, hd, hd), F32))
    kern = functools.partial(_scan_kernel, has_s0=s0 is not None, emit_o=emit_o, emit_s=emit_s)
    return pl.pallas_call(
        kern,
        grid=(bsz, n_heads, nb),
        in_specs=in_specs,
        out_specs=out_specs,
        out_shape=out_shape,
        scratch_shapes=[pltpu.VMEM((2, hd, hd), F32)],
        compiler_params=_params("arbitrary", "arbitrary", "arbitrary"),
        name=name,
    )(*args)


def _rope_tables(length):
    half = HEAD_DIM // 2
    rows = length // GRID_W
    row = jnp.repeat(jnp.arange(rows), GRID_W).astype(F32)
    col = jnp.tile(jnp.arange(GRID_W), rows).astype(F32)
    inv = ROPE_THETA ** (-jnp.arange(0, half, 2, dtype=F32) / half)
    def axis_angles(pos):
        a = pos[:, None] * inv[None, :]
        return jnp.concatenate([a, a], axis=-1)
    ang = jnp.concatenate([axis_angles(row), axis_angles(col)], axis=-1)
    ang = jnp.concatenate([ang, ang], axis=-1)
    first_half = (jnp.arange(HEAD_LANES) % half) < half // 2
    sin = jnp.sin(ang)
    return jnp.cos(ang), jnp.where(first_half, -sin, 0.0), jnp.where(first_half, 0.0, sin)


def kernel(x, c, ctx, c_ctx, w_ada, b_ada, norm_g, w_in_ab, w_out_ab, qn_g, kn_g, lam_q1, lam_k1,
           lam_q2, lam_k2, subln_g, conv_w, conv_b, cln_g, cln_b, w_in_c, w_out_c, lb_gamma, onorm_g):
    bsz, seq, d = x.shape
    n_ctx = ctx.shape[1]
    depth = w_ada.shape[0]
    assert depth == 2 and w_in_ab.shape[0] == 1 and w_in_c.shape[0] == 1
    att_width = N_HEADS * HEAD_LANES
    conv_width = conv_w.shape[-1]
    assert w_in_ab.shape[-1] == 4 * att_width + 3 * conv_width and conv_width == att_width

    cond_rows = jnp.concatenate([c, c_ctx[None, :], jnp.zeros((8 - bsz - 1, d), F32)], axis=0)
    mod = _ada(cond_rows, w_ada, b_ada)
    def mods(l, rows):
        m = mod[l, rows][:, None, :]
        return m[..., :d], m[..., d:2 * d], m[..., 2 * d:]

    cos, sin_a, sin_b = _rope_tables(seq)
    gmat = jnp.kron(jnp.eye(2, dtype=F32), jnp.full((HEAD_DIM, HEAD_DIM), 1.0 / HEAD_DIM, F32)
                    ).astype(BF16)
    two = lambda g: jnp.concatenate([g, g]).reshape(1, HEAD_LANES)
    q_scale = HEAD_DIM ** -0.5

    x2d = x.reshape(bsz * seq, d)
    ctx2d = ctx.reshape(bsz * n_ctx, d)
    lat_rows = slice(0, bsz)
    ctx_rows = slice(bsz, bsz + 1)

    lam_init = 0.8 - 0.6 * math.exp(-0.3 * 0)
    lam = (jnp.exp(jnp.sum(lam_q1[0] * lam_k1[0])) - jnp.exp(jnp.sum(lam_q2[0] * lam_k2[0]))
           + lam_init).reshape(1)
    shift, scale, gate = mods(0, lat_rows)
    shift_c, scale_c, gate_c = mods(0, ctx_rows)
    w_in = w_in_ab[0].astype(BF16)
    w_out = w_out_ab[0].astype(BF16)
    ab_in = w_in.shape[1]
    qk_lat = (two(qn_g[0]), two(kn_g[0]), gmat, cos, sin_a, sin_b, att_width, q_scale)
    qk_ctx = (two(qn_g[0]), two(kn_g[0]), gmat, None, None, None, att_width, q_scale)
    gates = ((3 * att_width, 4 * att_width), (ab_in - conv_width, ab_in))
    p_lat = _proj(x2d, norm_g[0], scale, shift, w_in, seq, tm=1024, tn=1024, qk=qk_lat,
                  silu_cols=gates, name="proj_ab_lat").reshape(bsz, seq, ab_in)
    p_ctx = _proj(ctx2d, norm_g[0], scale_c, shift_c, w_in, bsz * n_ctx, tm=1024, tn=512, qk=qk_ctx,
                  silu_cols=gates, name="proj_ab_ctx").reshape(bsz, n_ctx, ab_in)
    bound = math.sqrt(HEAD_DIM) * jnp.max(jnp.abs(qn_g[0])) * jnp.max(jnp.abs(kn_g[0]))
    scal = jnp.concatenate([lam, bound.reshape(1)])
    def attend(bounded):
        def run():
            lat = _attention(scal, p_lat, p_ctx, p_lat, subln_g[0], tq=1024, att_width=att_width,
                             lam_init=lam_init, bounded=bounded,
                             name="attn_lat_bounded" if bounded else "attn_lat")
            ctx_ = _attention(scal, p_ctx, p_ctx, None, subln_g[0], tq=n_ctx, att_width=att_width,
                              lam_init=lam_init, bounded=bounded,
                              name="attn_ctx_bounded" if bounded else "attn_ctx")
            return lat, ctx_
        return run
    att_lat, att_ctx = lax.cond(bound <= MAX_FIXED_SHIFT, attend(True), attend(False))
    conv_col0 = 4 * att_width // conv_width
    conv_lat = _conv_branch(p_lat, conv_w[0], conv_b[0], cln_g[0], cln_b[0], tl=256,
                            col0=conv_col0, width=conv_width, name="conv_lat")
    conv_ctx = _conv_branch(p_ctx, conv_w[0], conv_b[0], cln_g[0], cln_b[0], tl=256,
                            col0=conv_col0, width=conv_width, name="conv_ctx")
    x2d = _out_ab(att_lat.reshape(bsz * seq, att_width), conv_lat.reshape(bsz * seq, conv_width),
                  w_out, x2d, gate, seq, tm=1024, tn=1024, name="out_ab_lat")
    ctx2d = _out_ab(att_ctx.reshape(bsz * n_ctx, att_width),
                    conv_ctx.reshape(bsz * n_ctx, conv_width),
                    w_out, ctx2d, gate_c, bsz * n_ctx, tm=512, tn=1024, name="out_ab_ctx")

    p = jax.nn.softmax(lb_gamma.astype(F32), axis=1)
    lb_all = jnp.cumsum(p, axis=1) - p[:, :1]
    lb = lb_all[:, 1].reshape(2, 1, d)
    shift, scale, gate = mods(1, lat_rows)
    shift_c, scale_c, _ = mods(1, ctx_rows)
    w_in = w_in_c[0].astype(BF16)
    w_out = w_out_c[0].astype(BF16)
    c_in = w_in.shape[1]
    gates = ((0, d), (4 * d, 5 * d))
    p_lat = _proj(x2d, norm_g[1], scale, shift, w_in, seq, tm=1024, tn=1024, silu_cols=gates,
                  name="proj_c_lat").reshape(bsz, seq, c_in)
    p_ctx = _proj(ctx2d, norm_g[1], scale_c, shift_c, w_in, bsz * n_ctx, tm=1024, tn=512,
                  silu_cols=gates, name="proj_c_ctx").reshape(bsz, n_ctx, c_in)
    s_ctx, = _scan(p_ctx, lb, None, tb=n_ctx, n_heads=HGRN_HEADS, emit_o=False, emit_s=True,
                   name="scan_ctx")
    o_f, o_b = _scan(p_lat, lb, s_ctx, tb=1024, n_heads=HGRN_HEADS, emit_o=True, emit_s=False,
                     name="scan_lat")
    out = _out_c(o_f.reshape(bsz * seq, d), o_b.reshape(bsz * seq, d),
                 p_lat.reshape(bsz * seq, c_in), 4, onorm_g[0], w_out, x2d, gate, seq,
                 tm=1024, tn=512, name="out_c_lat")
    return out.reshape(bsz, seq, d)
```

```python
import functools
import math

import jax
import jax.numpy as jnp
from jax import lax
from jax.experimental import pallas as pl
from jax.experimental.pallas import tpu as pltpu

F32 = jnp.float32
BF16 = jnp.bfloat16

EPS = 1e-6
GRID_W = 64
ROPE_THETA = 10000.0
HEAD_DIM = 64
HEAD_LANES = 128
N_HEADS = 8
CONV_K = 31
CONV_HALO = 16
SUBLANES = 8
HGRN_HEADS = 16
HGRN_CHUNK = 64
SCAN_BLOCK = 256

KEY_CHUNK = 256
Q_SUB = 256
MAX_FIXED_SHIFT = 40.0
VMEM_LIMIT = 48 * 1024 * 1024


def _silu(x):
    return x * jax.nn.sigmoid(x)


def _params(*sem):
    return pltpu.CompilerParams(dimension_semantics=sem, vmem_limit_bytes=VMEM_LIMIT)


def _ada_kernel(c_ref, w_ref, b_ref, o_ref):
    sc = _silu(c_ref[...])
    o_ref[0] = jnp.dot(sc, w_ref[0], precision=lax.Precision.HIGHEST,
                       preferred_element_type=F32) + b_ref[0]


def _ada(cond_rows, w_ada, b_ada):
    depth, d, n = w_ada.shape
    tn = 768
    return pl.pallas_call(
        _ada_kernel,
        grid=(depth, n // tn),
        in_specs=[pl.BlockSpec((8, d), lambda l, j: (0, 0)),
                  pl.BlockSpec((1, d, tn), lambda l, j: (l, 0, j)),
                  pl.BlockSpec((1, 1, tn), lambda l, j: (l, 0, j))],
        out_specs=pl.BlockSpec((1, 8, tn), lambda l, j: (l, 0, j)),
        out_shape=jax.ShapeDtypeStruct((depth, 8, n), F32),
        compiler_params=_params("arbitrary", "arbitrary"),
        name="ada",
    )(cond_rows, w_ada, b_ada.reshape(depth, 1, n))


def _group_mean_sq(x, gmat):
    sq = x * x
    hi = sq.astype(BF16)
    lo = (sq - hi.astype(F32)).astype(BF16)
    return (jnp.dot(hi, gmat, preferred_element_type=F32)
            + jnp.dot(lo, gmat, preferred_element_type=F32))


def _proj_kernel(*refs, n_qk_tiles, silu_tiles, rope, q_scale, row_chunk, ep_rows):
    if n_qk_tiles:
        (x_ref, g_ref, sc_ref, sh_ref, w_ref, qg_ref, kg_ref, gmat_ref,
         cos_ref, sa_ref, sb_ref, o_ref, vt_ref, xn_ref, acc_ref) = refs
    else:
        x_ref, g_ref, sc_ref, sh_ref, w_ref, o_ref, xn_ref = refs
    j = pl.program_id(1)
    tm, tn = o_ref.shape

    @pl.when(j == 0)
    def _normalise():
        gain = g_ref[...] * (1.0 + sc_ref[0])
        shift = sh_ref[0]

        def body(r, carry):
            r0 = pl.multiple_of(r * row_chunk, row_chunk)
            x = x_ref[pl.ds(r0, row_chunk), :]
            ms = jnp.mean(x * x, axis=-1, keepdims=True)
            xn_ref[pl.ds(r0, row_chunk), :] = (x * lax.rsqrt(ms + EPS) * gain + shift).astype(BF16)
            return carry

        lax.fori_loop(0, tm // row_chunk, body, 0, unroll=2)

    def matmul():
        return jnp.dot(xn_ref[...], w_ref[...], preferred_element_type=F32)

    def norm_rope(gain_ref, scale):
        acc_ref[...] = matmul()
        gmat = gmat_ref[...]
        gain = gain_ref[...]

        def body(r, carry):
            rows = pl.ds(pl.multiple_of(r * ep_rows, ep_rows), ep_rows)
            for c in range(tn // HEAD_LANES):
                cols = slice(c * HEAD_LANES, (c + 1) * HEAD_LANES)
                xs = acc_ref[rows, cols]
                y = xs * lax.rsqrt(_group_mean_sq(xs, gmat) + EPS) * gain
                if rope:
                    y = (y * cos_ref[rows, :]
                         + pltpu.roll(y, HEAD_LANES - 16, axis=1) * sa_ref[rows, :]
                         + pltpu.roll(y, 16, axis=1) * sb_ref[rows, :])
                if scale != 1.0:
                    y = y * scale
                o_ref[rows, cols] = y.astype(BF16)
            return carry

        lax.fori_loop(0, tm // ep_rows, body, 0)

    plain = j >= 2 * n_qk_tiles
    if n_qk_tiles:
        @pl.when(j < n_qk_tiles)
        def _q():
            norm_rope(qg_ref, q_scale)

        @pl.when((j >= n_qk_tiles) & (j < 2 * n_qk_tiles))
        def _k():
            norm_rope(kg_ref, 1.0)

        plain = j > 2 * n_qk_tiles

        @pl.when(j == 2 * n_qk_tiles)
        def _values():
            acc = matmul()
            o_ref[...] = acc.astype(BF16)
            n_seq, n_heads, chunks_per_seq = vt_ref.shape[:3]
            for s in range(n_seq):
                for c in range(chunks_per_seq):
                    r0 = (s * chunks_per_seq + c) * KEY_CHUNK
                    for h in range(n_heads):
                        blk = acc[r0:r0 + KEY_CHUNK, h * HEAD_LANES:(h + 1) * HEAD_LANES]
                        vt_ref[s, h, c] = blk.T.astype(BF16)

    gated = functools.reduce(lambda a, b: a | b, [j == t for t in silu_tiles], j < 0)

    @pl.when(plain & gated)
    def _gated():
        o_ref[...] = _silu(matmul()).astype(BF16)

    @pl.when(plain & jnp.logical_not(gated))
    def _rest():
        o_ref[...] = matmul().astype(BF16)


def _proj(x2d, gain, scale, shift, w_bf, rows_per_group, *, tm, tn, qk=None, silu_cols=(), seq_len=None,
          name):
    m, d = x2d.shape
    n = w_bf.shape[1]
    tiles_per_group = rows_per_group // tm
    in_specs = [
        pl.BlockSpec((tm, d), lambda i, j: (i, 0)),
        pl.BlockSpec((1, d), lambda i, j: (0, 0)),
        pl.BlockSpec((1, 1, d), lambda i, j: (i // tiles_per_group, 0, 0)),
        pl.BlockSpec((1, 1, d), lambda i, j: (i // tiles_per_group, 0, 0)),
        pl.BlockSpec((d, tn), lambda i, j: (0, j)),
    ]
    args = [x2d, gain.reshape(1, d), scale, shift, w_bf]
    scratch = [pltpu.VMEM((tm, d), BF16)]
    out_specs = [pl.BlockSpec((tm, tn), lambda i, j: (i, j))]
    out_shape = [jax.ShapeDtypeStruct((m, n), BF16)]
    n_qk_tiles, rope, q_scale = 0, False, 1.0
    if qk is not None:
        qg, kg, gmat, cos, sa, sb, q_width, q_scale = qk
        n_qk_tiles = q_width // tn
        rope = cos is not None
        const = lambda i, j: (0, 0)
        in_specs += [pl.BlockSpec((1, HEAD_LANES), const), pl.BlockSpec((1, HEAD_LANES), const),
                     pl.BlockSpec((HEAD_LANES, HEAD_LANES), const)]
        args += [qg, kg, gmat]
        if rope:
            pos_tiles = cos.shape[0] // tm
            pos = lambda i, j: (i % pos_tiles, 0)
            in_specs += [pl.BlockSpec((tm, HEAD_LANES), pos)] * 3
            args += [cos, sa, sb]
        else:
            dummy = jnp.zeros((8, HEAD_LANES), F32)
            in_specs += [pl.BlockSpec((8, HEAD_LANES), const)] * 3
            args += [dummy, dummy, dummy]
        scratch.append(pltpu.VMEM((tm, tn), F32))
        assert tn == q_width
        heads = tn // HEAD_LANES
        seq_rows = min(tm, seq_len)
        tiles_per_seq = seq_len // seq_rows
        vt_block = (tm // seq_rows, heads, seq_rows // KEY_CHUNK, HEAD_LANES, KEY_CHUNK)
        out_specs.append(pl.BlockSpec(
            vt_block, lambda i, j: (i // tiles_per_seq, 0, i % tiles_per_seq, 0, 0)))
        out_shape.append(jax.ShapeDtypeStruct(
            (m // seq_len, heads, seq_len // KEY_CHUNK, HEAD_LANES, KEY_CHUNK), BF16))
    silu_tiles = tuple(t for a, b in silu_cols for t in range(a // tn, b // tn))
    kern = functools.partial(_proj_kernel, n_qk_tiles=n_qk_tiles, silu_tiles=silu_tiles,
                             rope=rope, q_scale=q_scale, row_chunk=32, ep_rows=256)
    outs = pl.pallas_call(
        kern,
        grid=(m // tm, n // tn),
        in_specs=in_specs,
        out_specs=out_specs,
        out_shape=out_shape,
        scratch_shapes=scratch,
        compiler_params=_params("arbitrary", "arbitrary"),
        name=name,
    )(*args)
    return outs if qk is not None else outs[0]


def _attn_refs(refs, has_lat):
    if has_lat:
        return refs
    scal_ref, q_ref, kc_ref, vtc_ref, za_ref, sg_ref, o_ref = refs
    return scal_ref, q_ref, kc_ref, vtc_ref, None, None, za_ref, sg_ref, o_ref


def _key_value_chunks(kc_ref, vtc_ref, kl_ref, vtl_ref):
    n_ctx_chunks = vtc_ref.shape[2]
    n_lat_chunks = 0 if vtl_ref is None else vtl_ref.shape[2]

    def ctx(c):
        return kc_ref[0, c * KEY_CHUNK:(c + 1) * KEY_CHUNK, :], vtc_ref[0, 0, c]

    def lat(c):
        return kl_ref[0, pl.ds(pl.multiple_of(c * KEY_CHUNK, KEY_CHUNK), KEY_CHUNK), :], vtl_ref[0, 0, c]

    return n_ctx_chunks, n_lat_chunks, ctx, lat


def _query_maps(q):
    lane = lax.broadcasted_iota(jnp.int32, q.shape, 1)
    zero = jnp.zeros_like(q)
    return jnp.where(lane < HEAD_DIM, q, zero), jnp.where(lane >= HEAD_DIM, q, zero)


def _attn_finish(o_t, za, sg_ref, lam_init):
    o = o_t.T
    ms = jnp.mean(o * o, axis=-1, keepdims=True)
    y = o * lax.rsqrt(ms + EPS) * sg_ref[...] * (1.0 - lam_init)
    return (y * za.astype(F32)).astype(BF16)


def _attn_kernel(*refs, has_lat, lam_init):
    (scal_ref, q_ref, kc_ref, vtc_ref, kl_ref, vtl_ref, za_ref, sg_ref,
     o_ref) = _attn_refs(refs, has_lat)
    tq = q_ref.shape[1]
    n_ctx_chunks, n_lat_chunks, ctx_chunk, lat_chunk = _key_value_chunks(kc_ref, vtc_ref, kl_ref, vtl_ref)
    q_maps = _query_maps(q_ref[0])

    def step(kblk, vtb, carry):
        out = []
        for mp in range(2):
            m_old, l_old, o_old = carry[3 * mp:3 * mp + 3]
            s = lax.dot_general(kblk, q_maps[mp], (((1,), (1,)), ((), ())),
                                preferred_element_type=F32)
            m_new = jnp.maximum(m_old, jnp.max(s, axis=0, keepdims=True))
            p = jnp.exp(s - m_new)
            alpha = jnp.exp(m_old - m_new)
            l_new = alpha * l_old + jnp.sum(p, axis=0, keepdims=True)
            o_new = alpha * o_old + jnp.dot(vtb, p.astype(BF16), preferred_element_type=F32)
            out += [m_new, l_new, o_new]
        return tuple(out)

    m0 = jnp.full((1, tq), -1e30, F32)
    l0 = jnp.zeros((1, tq), F32)
    o0 = jnp.zeros((HEAD_LANES, tq), F32)
    carry = (m0, l0, o0, m0, l0, o0)
    for c in range(n_ctx_chunks):
        carry = step(*ctx_chunk(c), carry)
    if n_lat_chunks:
        carry = lax.fori_loop(0, n_lat_chunks, lambda c, acc: step(*lat_chunk(c), acc), carry)
    _, l_a, o_a, _, l_b, o_b = carry
    o_ref[0] = _attn_finish(o_a / l_a - scal_ref[0] * (o_b / l_b), za_ref[0], sg_ref, lam_init)


def _attn_bounded_kernel(*refs, has_lat, lam_init):
    (scal_ref, q_ref, kc_ref, vtc_ref, kl_ref, vtl_ref, za_ref, sg_ref,
     o_ref) = _attn_refs(refs, has_lat)
    tq = q_ref.shape[1]
    n_ctx_chunks, n_lat_chunks, ctx_chunk, lat_chunk = _key_value_chunks(kc_ref, vtc_ref, kl_ref, vtl_ref)
    n_chunks = n_ctx_chunks + n_lat_chunks

    def chunk(c):
        return ctx_chunk(c) if c < n_ctx_chunks else lat_chunk(c - n_ctx_chunks)

    shift = scal_ref[1]
    steps = [(t, c) for t in range(tq // Q_SUB) for c in range(n_chunks)]
    q_maps = {}

    def scores(t, c):
        if t not in q_maps:
            q_maps[t] = _query_maps(q_ref[0, t * Q_SUB:(t + 1) * Q_SUB, :])
        kblk = chunk(c)[0]
        return [lax.dot_general(kblk, q_maps[t][mp], (((1,), (1,)), ((), ())),
                                preferred_element_type=F32) for mp in range(2)]

    s_next = scores(*steps[0])
    for i, (t, c) in enumerate(steps):
        if c == 0:
            l_acc = [jnp.zeros((SUBLANES, Q_SUB), F32) for _ in range(2)]
            o_acc = [jnp.zeros((HEAD_LANES, Q_SUB), F32) for _ in range(2)]
        s_cur = s_next
        if i + 1 < len(steps):
            s_next = scores(*steps[i + 1])
        vtb = chunk(c)[1]
        for mp in range(2):
            p = jnp.exp(s_cur[mp] - shift)
            l_acc[mp] = l_acc[mp] + jnp.sum(p.reshape(KEY_CHUNK // SUBLANES, SUBLANES, Q_SUB), axis=0)
            o_acc[mp] = o_acc[mp] + jnp.dot(vtb, p.astype(BF16), preferred_element_type=F32)
        if c == n_chunks - 1:
            l_a = jnp.sum(l_acc[0], axis=0, keepdims=True)
            l_b = jnp.sum(l_acc[1], axis=0, keepdims=True)
            rows = slice(t * Q_SUB, (t + 1) * Q_SUB)
            o_ref[0, rows, :] = _attn_finish(o_acc[0] / l_a - scal_ref[0] * (o_acc[1] / l_b),
                                             za_ref[0, rows, :], sg_ref, lam_init)


def _attention(scal, proj_q, proj_ctx, vt_ctx, proj_lat, vt_lat, subln_g, *, tq, att_width, lam_init,
               bounded, name):
    bsz, lq, _ = proj_q.shape
    n_ctx = proj_ctx.shape[1]
    nh = att_width // HEAD_LANES
    k_off, z_off = nh, 3 * nh
    vt_spec = lambda vt: pl.BlockSpec((1, 1) + vt.shape[2:], lambda b, h, i: (b, h, 0, 0, 0))
    in_specs = [
        pl.BlockSpec(memory_space=pltpu.SMEM),
        pl.BlockSpec((1, tq, HEAD_LANES), lambda b, h, i: (b, i, h)),
        pl.BlockSpec((1, n_ctx, HEAD_LANES), lambda b, h, i: (b, 0, k_off + h)),
        vt_spec(vt_ctx),
    ]
    args = [scal, proj_q, proj_ctx, vt_ctx]
    if proj_lat is not None:
        n_lat = proj_lat.shape[1]
        in_specs += [pl.BlockSpec((1, n_lat, HEAD_LANES), lambda b, h, i: (b, 0, k_off + h)),
                     vt_spec(vt_lat)]
        args += [proj_lat, vt_lat]
    in_specs += [pl.BlockSpec((1, tq, HEAD_LANES), lambda b, h, i: (b, i, z_off + h)),
                 pl.BlockSpec((1, HEAD_LANES), lambda b, h, i: (0, 0))]
    args += [proj_q, subln_g.reshape(1, HEAD_LANES)]
    kern = functools.partial(_attn_bounded_kernel if bounded else _attn_kernel,
                             has_lat=proj_lat is not None, lam_init=lam_init)
    return pl.pallas_call(
        kern,
        grid=(bsz, nh, lq // tq),
        in_specs=in_specs,
        out_specs=pl.BlockSpec((1, tq, HEAD_LANES), lambda b, h, i: (b, i, h)),
        out_shape=jax.ShapeDtypeStruct((bsz, lq, att_width), BF16),
        compiler_params=_params("arbitrary", "arbitrary", "arbitrary"),
        name=name,
    )(*args)


def _conv_kernel(gv_ref, gg_ref, zb_ref, gvp_ref, ggp_ref, gvn_ref, ggn_ref,
                 w_ref, b_ref, lg_ref, lb_ref, o_ref, ypad_ref, ysh_ref, *, row_chunk):
    i = pl.program_id(1)
    n_tiles = pl.num_programs(1)
    tl = o_ref.shape[1]
    padded = tl + 2 * CONV_HALO

    def glu(val_ref, gate_ref, rows):
        return val_ref[0, rows, :].astype(F32) * jax.nn.sigmoid(gate_ref[0, rows, :].astype(F32))

    halo = slice(0, CONV_HALO)
    ypad_ref[0:CONV_HALO, :] = jnp.where(i > 0, glu(gvp_ref, ggp_ref, halo), 0.0)
    ypad_ref[CONV_HALO + tl:padded, :] = jnp.where(i < n_tiles - 1, glu(gvn_ref, ggn_ref, halo), 0.0)
    for r in range(tl // row_chunk):
        rows = slice(r * row_chunk, (r + 1) * row_chunk)
        ypad_ref[CONV_HALO + r * row_chunk:CONV_HALO + (r + 1) * row_chunk, :] = glu(gv_ref, gg_ref, rows)

    span = padded - SUBLANES
    for s in range(1, SUBLANES):
        for r in range(0, span, row_chunk):
            n = min(row_chunk, span - r)
            ysh_ref[s, r:r + n, :] = ypad_ref[r + s:r + s + n, :]

    first = CONV_HALO - CONV_K // 2
    for r in range(tl // row_chunk):
        acc = None
        for t in range(CONV_K):
            start = r * row_chunk + first + t
            phase, base = start % SUBLANES, start - start % SUBLANES
            src = ypad_ref[base:base + row_chunk, :] if phase == 0 else ysh_ref[phase, base:base + row_chunk, :]
            term = src.reshape(row_chunk // SUBLANES, SUBLANES, -1) * w_ref[t][None]
            acc = term if acc is None else acc + term
        acc = acc.reshape(row_chunk, -1) + b_ref[...]
        mu = jnp.mean(acc, axis=-1, keepdims=True)
        xc = acc - mu
        var = jnp.mean(xc * xc, axis=-1, keepdims=True)
        y = xc * lax.rsqrt(var + EPS) * lg_ref[...] + lb_ref[...]
        rows = slice(r * row_chunk, (r + 1) * row_chunk)
        o_ref[0, rows, :] = (_silu(y) * zb_ref[0, rows, :].astype(F32)).astype(BF16)


def _conv_branch(proj, conv_w, conv_b, cln_g, cln_b, *, tl, col0, width, name):
    bsz, length, _ = proj.shape
    n_tiles = length // tl
    halo_per_tile = tl // CONV_HALO
    last_halo = length // CONV_HALO - 1
    main = lambda off: pl.BlockSpec((1, tl, width), lambda b, i: (b, i, col0 + off))
    prev = lambda off: pl.BlockSpec(
        (1, CONV_HALO, width), lambda b, i: (b, jnp.maximum(i * halo_per_tile - 1, 0), col0 + off))
    nxt = lambda off: pl.BlockSpec(
        (1, CONV_HALO, width),
        lambda b, i: (b, jnp.minimum((i + 1) * halo_per_tile, last_halo), col0 + off))
    row = lambda k: pl.BlockSpec((k, width), lambda b, i: (0, 0))
    padded = tl + 2 * CONV_HALO
    return pl.pallas_call(
        functools.partial(_conv_kernel, row_chunk=32),
        grid=(bsz, n_tiles),
        in_specs=[main(0), main(1), main(2), prev(0), prev(1), nxt(0), nxt(1),
                  pl.BlockSpec((CONV_K, SUBLANES, width), lambda b, i: (0, 0, 0)),
                  row(1), row(1), row(1)],
        out_specs=pl.BlockSpec((1, tl, width), lambda b, i: (b, i, 0)),
        out_shape=jax.ShapeDtypeStruct((bsz, length, width), BF16),
        scratch_shapes=[pltpu.VMEM((padded, width), F32),
                        pltpu.VMEM((SUBLANES, padded, width), F32)],
        compiler_params=_params("arbitrary", "arbitrary"),
        name=name,
    )(proj, proj, proj, proj, proj, proj, proj,
      jnp.broadcast_to(conv_w[:, None, :], (CONV_K, SUBLANES, width)),
      conv_b.reshape(1, width), cln_g.reshape(1, width), cln_b.reshape(1, width))


def _out_ab_kernel(a_ref, c_ref, wa_ref, wc_ref, x_ref, gate_ref, o_ref):
    y = (jnp.dot(a_ref[...], wa_ref[...], preferred_element_type=F32)
         + jnp.dot(c_ref[...], wc_ref[...], preferred_element_type=F32))
    o_ref[...] = x_ref[...] + gate_ref[0] * y


def _out_ab(att, conv, w_bf, x2d, gate, rows_per_group, *, tm, tn, name):
    m, d = x2d.shape
    ka = att.shape[1]
    kc = conv.shape[1]
    tiles_per_group = rows_per_group // tm
    return pl.pallas_call(
        _out_ab_kernel,
        grid=(m // tm, d // tn),
        in_specs=[pl.BlockSpec((tm, ka), lambda i, j: (i, 0)),
                  pl.BlockSpec((tm, kc), lambda i, j: (i, 0)),
                  pl.BlockSpec((ka, tn), lambda i, j: (0, j)),
                  pl.BlockSpec((kc, tn), lambda i, j: (ka // kc, j)),
                  pl.BlockSpec((tm, tn), lambda i, j: (i, j)),
                  pl.BlockSpec((1, 1, tn), lambda i, j: (i // tiles_per_group, 0, j))],
        out_specs=pl.BlockSpec((tm, tn), lambda i, j: (i, j)),
        out_shape=jax.ShapeDtypeStruct((m, d), F32),
        compiler_params=_params("arbitrary", "arbitrary"),
        name=name,
    )(att, conv, w_bf, w_bf, x2d, gate)


def _out_c_kernel(of_ref, ob_ref, z_ref, g_ref, w_ref, x_ref, gate_ref, o_ref, y_ref, *, row_chunk):
    j = pl.program_id(1)
    tm, d = y_ref.shape

    @pl.when(j == 0)
    def _readout():
        def body(r, carry):
            rows = pl.ds(pl.multiple_of(r * row_chunk, row_chunk), row_chunk)
            for h in range(d // HEAD_LANES):
                cols = slice(h * HEAD_LANES, (h + 1) * HEAD_LANES)
                o = of_ref[rows, cols].astype(F32) + ob_ref[rows, cols].astype(F32)
                ms = jnp.mean(o * o, axis=-1, keepdims=True)
                y = o * lax.rsqrt(ms + EPS) * g_ref[...]
                y_ref[rows, cols] = (y * z_ref[rows, cols].astype(F32)).astype(BF16)
            return carry
        lax.fori_loop(0, tm // row_chunk, body, 0)

    y = jnp.dot(y_ref[...], w_ref[...], preferred_element_type=F32)
    o_ref[...] = x_ref[...] + gate_ref[0] * y


def _out_c(o_f, o_b, proj, z_col, onorm_g, w_bf, x2d, gate, rows_per_group, *, tm, tn, name):
    m, d = x2d.shape
    tiles_per_group = rows_per_group // tm
    return pl.pallas_call(
        functools.partial(_out_c_kernel, row_chunk=64),
        grid=(m // tm, d // tn),
        in_specs=[pl.BlockSpec((tm, d), lambda i, j: (i, 0)),
                  pl.BlockSpec((tm, d), lambda i, j: (i, 0)),
                  pl.BlockSpec((tm, d), lambda i, j: (i, z_col)),
                  pl.BlockSpec((1, HEAD_LANES), lambda i, j: (0, 0)),
                  pl.BlockSpec((d, tn), lambda i, j: (0, j)),
                  pl.BlockSpec((tm, tn), lambda i, j: (i, j)),
                  pl.BlockSpec((1, 1, tn), lambda i, j: (i // tiles_per_group, 0, j))],
        out_specs=pl.BlockSpec((tm, tn), lambda i, j: (i, j)),
        out_shape=jax.ShapeDtypeStruct((m, d), F32),
        scratch_shapes=[pltpu.VMEM((tm, d), BF16)],
        compiler_params=_params("arbitrary", "arbitrary"),
        name=name,
    )(o_f, o_b, proj, onorm_g.reshape(1, HEAD_LANES), w_bf, x2d, gate)


def _chunk_running_sum(x, tri_bf):
    hi = x.astype(BF16)
    lo = (x - hi.astype(F32)).astype(BF16)
    return (jnp.dot(tri_bf, hi, preferred_element_type=F32)
            + jnp.dot(tri_bf, lo, preferred_element_type=F32))


def _scan_kernel(*refs, has_s0, emit_o, emit_s):
    refs = list(refs)
    qf_ref, vf_ref, uf_ref, qb_ref, vb_ref, ub_ref, lb_ref, tri_ref = refs[:8]
    refs = refs[8:]
    s0_ref = refs.pop(0) if has_s0 else None
    of_ref, ob_ref = (refs.pop(0), refs.pop(0)) if emit_o else (None, None)
    sout_ref = refs.pop(0) if emit_s else None
    st_ref, = refs
    n = pl.program_id(2)
    tb = qf_ref.shape[1]

    @pl.when(n == 0)
    def _init():
        if has_s0:
            st_ref[...] = s0_ref[:, 0, 0]
        else:
            st_ref[...] = jnp.zeros_like(st_ref)

    sources = ((qf_ref, vf_ref, uf_ref, of_ref), (qb_ref, vb_ref, ub_ref, ob_ref))
    blk = tri_ref.shape[1]
    n_sub = tb // blk
    n_chunks = blk // HGRN_CHUNK
    work = []
    for d in range(2):
        for sb in (range(n_sub - 1, -1, -1) if d == 1 else range(n_sub)):
            work.append(dict(d=d, rows=slice(sb * blk, (sb + 1) * blk), base=sb * blk))

    for w in work:
        d = w["d"]
        lbd = lb_ref[d]
        sig = jax.nn.sigmoid(sources[d][2][0, w["rows"], :].astype(F32))
        logf = jnp.log(lbd + (1.0 - lbd) * sig)
        w["k"] = (1.0 - lbd) * (1.0 - sig)
        w["b"] = _chunk_running_sum(logf, tri_ref[d].astype(BF16))

    for w in work:
        d = w["d"]
        q_ref, v_ref = sources[d][:2]
        decay = jnp.exp(w["b"])
        k_inv = w["k"] / decay
        v_bf = v_ref[0, w["rows"], :]
        q_dec = (q_ref[0, w["rows"], :].astype(F32) * decay).astype(BF16)
        if emit_o:
            w["a"] = lax.dot_general(q_dec, k_inv.astype(BF16), (((1,), (1,)), ((), ())),
                                     preferred_element_type=F32)
        decay_t = decay.T
        chunks = []
        for c in (range(n_chunks - 1, -1, -1) if d == 1 else range(n_chunks)):
            rows = slice(c * HGRN_CHUNK, (c + 1) * HGRN_CHUNK)
            last = c * HGRN_CHUNK if d == 1 else (c + 1) * HGRN_CHUNK - 1
            k_end = (k_inv[rows] * decay[last:last + 1, :]).astype(BF16)
            kv = lax.dot_general(k_end, v_bf[rows], (((0,), (0,)), ((), ())),
                                 preferred_element_type=F32)
            chunks.append((rows, decay_t[:, last:last + 1], kv))
        w.update(q_dec=q_dec, v_bf=v_bf, chunks=chunks)

    if emit_o:
        for w in work:
            a = jnp.where(tri_ref[w["d"]] > 0.5, w["a"], 0.0).astype(BF16)
            w["o_intra"] = jnp.dot(a, w["v_bf"], preferred_element_type=F32)

    readouts = [[], []]
    for d in range(2):
        st = st_ref[d]
        for w in work:
            if w["d"] != d:
                continue
            for rows, decay_end, kv in w["chunks"]:
                readouts[d].append((w, rows, st.astype(BF16)))
                st = decay_end * st + kv
        st_ref[d] = st

    if emit_o:
        for pair in zip(*readouts):
            for d, (w, rows, st_bf) in enumerate(pair):
                o = w["o_intra"][rows] + jnp.dot(w["q_dec"][rows], st_bf, preferred_element_type=F32)
                sources[d][3][0, w["base"] + rows.start:w["base"] + rows.stop, :] = o.astype(BF16)

    if emit_s:
        @pl.when(n == pl.num_programs(2) - 1)
        def _final():
            sout_ref[:, 0, 0] = st_ref[...]


def _scan(proj, lb, s0, *, tb, n_heads, emit_o, emit_s, name):
    bsz, length, _ = proj.shape
    nb = length // tb
    hd = HEAD_LANES
    fwd = lambda off: pl.BlockSpec((1, tb, hd), lambda b, h, n: (b, n, off * n_heads + h))
    bwd = lambda off: pl.BlockSpec((1, tb, hd), lambda b, h, n: (b, nb - 1 - n, off * n_heads + h))
    state = pl.BlockSpec((2, 1, 1, hd, hd), lambda b, h, n: (0, b, h, 0, 0))
    blk = min(tb, SCAN_BLOCK)
    tok = jnp.arange(blk)
    same_chunk = (tok[:, None] // HGRN_CHUNK) == (tok[None, :] // HGRN_CHUNK)
    tri = jnp.stack([same_chunk & (tok[None, :] <= tok[:, None]),
                     same_chunk & (tok[None, :] >= tok[:, None])]).astype(F32)
    in_specs = [fwd(0), fwd(1), fwd(2), bwd(0), bwd(1), bwd(3),
                pl.BlockSpec((2, 1, hd), lambda b, h, n: (0, 0, h)),
                pl.BlockSpec((2, blk, blk), lambda b, h, n: (0, 0, 0))]
    args = [proj] * 6 + [lb, tri]
    if s0 is not None:
        in_specs.append(state)
        args.append(s0)
    out_specs, out_shape = [], []
    if emit_o:
        out_specs += [pl.BlockSpec((1, tb, hd), lambda b, h, n: (b, n, h)),
                      pl.BlockSpec((1, tb, hd), lambda b, h, n: (b, nb - 1 - n, h))]
        out_shape += [jax.ShapeDtypeStruct((bsz, length, n_heads * hd), BF16)] * 2
    if emit_s:
        out_specs.append(state)
        out_shape.append(jax.ShapeDtypeStruct((2, bsz, n_heads, hd, hd), F32))
    kern = functools.partial(_scan_kernel, has_s0=s0 is not None, emit_o=emit_o, emit_s=emit_s)
    return pl.pallas_call(
        kern,
        grid=(bsz, n_heads, nb),
        in_specs=in_specs,
        out_specs=out_specs,
        out_shape=out_shape,
        scratch_shapes=[pltpu.VMEM((2, hd, hd), F32)],
        compiler_params=_params("arbitrary", "arbitrary", "arbitrary"),
        name=name,
    )(*args)


def _rope_tables(length):
    half = HEAD_DIM // 2
    rows = length // GRID_W
    row = jnp.repeat(jnp.arange(rows), GRID_W).astype(F32)
    col = jnp.tile(jnp.arange(GRID_W), rows).astype(F32)
    inv = ROPE_THETA ** (-jnp.arange(0, half, 2, dtype=F32) / half)
    def axis_angles(pos):
        a = pos[:, None] * inv[None, :]
        return jnp.concatenate([a, a], axis=-1)
    ang = jnp.concatenate([axis_angles(row), axis_angles(col)], axis=-1)
    ang = jnp.concatenate([ang, ang], axis=-1)
    first_half = (jnp.arange(HEAD_LANES) % half) < half // 2
    sin = jnp.sin(ang)
    return jnp.cos(ang), jnp.where(first_half, -sin, 0.0), jnp.where(first_half, 0.0, sin)


def kernel(x, c, ctx, c_ctx, w_ada, b_ada, norm_g, w_in_ab, w_out_ab, qn_g, kn_g, lam_q1, lam_k1,
           lam_q2, lam_k2, subln_g, conv_w, conv_b, cln_g, cln_b, w_in_c, w_out_c, lb_gamma, onorm_g):
    bsz, seq, d = x.shape
    n_ctx = ctx.shape[1]
    depth = w_ada.shape[0]
    assert depth == 2 and w_in_ab.shape[0] == 1 and w_in_c.shape[0] == 1
    att_width = N_HEADS * HEAD_LANES
    conv_width = conv_w.shape[-1]
    assert w_in_ab.shape[-1] == 4 * att_width + 3 * conv_width and conv_width == att_width

    cond_rows = jnp.concatenate([c, c_ctx[None, :], jnp.zeros((8 - bsz - 1, d), F32)], axis=0)
    mod = _ada(cond_rows, w_ada, b_ada)
    def mods(l, rows):
        m = mod[l, rows][:, None, :]
        return m[..., :d], m[..., d:2 * d], m[..., 2 * d:]

    cos, sin_a, sin_b = _rope_tables(seq)
    gmat = jnp.kron(jnp.eye(2, dtype=F32), jnp.full((HEAD_DIM, HEAD_DIM), 1.0 / HEAD_DIM, F32)
                    ).astype(BF16)
    two = lambda g: jnp.concatenate([g, g]).reshape(1, HEAD_LANES)
    q_scale = HEAD_DIM ** -0.5

    x2d = x.reshape(bsz * seq, d)
    ctx2d = ctx.reshape(bsz * n_ctx, d)
    lat_rows = slice(0, bsz)
    ctx_rows = slice(bsz, bsz + 1)

    lam_init = 0.8 - 0.6 * math.exp(-0.3 * 0)
    lam = (jnp.exp(jnp.sum(lam_q1[0] * lam_k1[0])) - jnp.exp(jnp.sum(lam_q2[0] * lam_k2[0]))
           + lam_init).reshape(1)
    shift, scale, gate = mods(0, lat_rows)
    shift_c, scale_c, gate_c = mods(0, ctx_rows)
    w_in = w_in_ab[0].astype(BF16)
    w_out = w_out_ab[0].astype(BF16)
    ab_in = w_in.shape[1]
    qk_lat = (two(qn_g[0]), two(kn_g[0]), gmat, cos, sin_a, sin_b, att_width, q_scale)
    qk_ctx = (two(qn_g[0]), two(kn_g[0]), gmat, None, None, None, att_width, q_scale)
    gates = ((3 * att_width, 4 * att_width), (ab_in - conv_width, ab_in))
    p_lat, vt_lat = _proj(x2d, norm_g[0], scale, shift, w_in, seq, tm=1024, tn=1024, qk=qk_lat,
                          silu_cols=gates, seq_len=seq, name="proj_ab_lat")
    p_ctx, vt_ctx = _proj(ctx2d, norm_g[0], scale_c, shift_c, w_in, bsz * n_ctx, tm=1024, tn=1024,
                          qk=qk_ctx, silu_cols=gates, seq_len=n_ctx, name="proj_ab_ctx")
    p_lat = p_lat.reshape(bsz, seq, ab_in)
    p_ctx = p_ctx.reshape(bsz, n_ctx, ab_in)
    bound = math.sqrt(HEAD_DIM) * jnp.max(jnp.abs(qn_g[0])) * jnp.max(jnp.abs(kn_g[0]))
    scal = jnp.concatenate([lam, bound.reshape(1)])
    def attend(bounded):
        def run():
            lat = _attention(scal, p_lat, p_ctx, vt_ctx, p_lat, vt_lat, subln_g[0], tq=1024,
                             att_width=att_width, lam_init=lam_init, bounded=bounded,
                             name="attn_lat_bounded" if bounded else "attn_lat")
            ctx_ = _attention(scal, p_ctx, p_ctx, vt_ctx, None, None, subln_g[0], tq=n_ctx,
                              att_width=att_width, lam_init=lam_init, bounded=bounded,
                              name="attn_ctx_bounded" if bounded else "attn_ctx")
            return lat, ctx_
        return run
    att_lat, att_ctx = lax.cond(bound <= MAX_FIXED_SHIFT, attend(True), attend(False))
    conv_col0 = 4 * att_width // conv_width
    conv_lat = _conv_branch(p_lat, conv_w[0], conv_b[0], cln_g[0], cln_b[0], tl=256,
                            col0=conv_col0, width=conv_width, name="conv_lat")
    conv_ctx = _conv_branch(p_ctx, conv_w[0], conv_b[0], cln_g[0], cln_b[0], tl=256,
                            col0=conv_col0, width=conv_width, name="conv_ctx")
    x2d = _out_ab(att_lat.reshape(bsz * seq, att_width), conv_lat.reshape(bsz * seq, conv_width),
                  w_out, x2d, gate, seq, tm=1024, tn=1024, name="out_ab_lat")
    ctx2d = _out_ab(att_ctx.reshape(bsz * n_ctx, att_width),
                    conv_ctx.reshape(bsz * n_ctx, conv_width),
                    w_out, ctx2d, gate_c, bsz * n_ctx, tm=512, tn=1024, name="out_ab_ctx")

    p = jax.nn.softmax(lb_gamma.astype(F32), axis=1)
    lb_all = jnp.cumsum(p, axis=1) - p[:, :1]
    lb = lb_all[:, 1].reshape(2, 1, d)
    shift, scale, gate = mods(1, lat_rows)
    shift_c, scale_c, _ = mods(1, ctx_rows)
    w_in = w_in_c[0].astype(BF16)
    w_out = w_out_c[0].astype(BF16)
    c_in = w_in.shape[1]
    gates = ((0, d), (4 * d, 5 * d))
    p_lat = _proj(x2d, norm_g[1], scale, shift, w_in, seq, tm=1024, tn=1024, silu_cols=gates,
                  name="proj_c_lat").reshape(bsz, seq, c_in)
    p_ctx = _proj(ctx2d, norm_g[1], scale_c, shift_c, w_in, bsz * n_ctx, tm=1024, tn=512,
                  silu_cols=gates, name="proj_c_ctx").reshape(bsz, n_ctx, c_in)
    s_ctx, = _scan(p_ctx, lb, None, tb=n_ctx, n_heads=HGRN_HEADS, emit_o=False, emit_s=True,
                   name="scan_ctx")
    o_f, o_b = _scan(p_lat, lb, s_ctx, tb=1024, n_heads=HGRN_HEADS, emit_o=True, emit_s=False,
                     name="scan_lat")
    out = _out_c(o_f.reshape(bsz * seq, d), o_b.reshape(bsz * seq, d),
                 p_lat.reshape(bsz * seq, c_in), 4, onorm_g[0], w_out, x2d, gate, seq,
                 tm=512, tn=2048, name="out_c_lat")
    return out.reshape(bsz, seq, d)
```

```python
import functools
import math

import jax
import jax.numpy as jnp
from jax import lax
from jax.experimental import pallas as pl
from jax.experimental.pallas import tpu as pltpu

F32 = jnp.float32
BF16 = jnp.bfloat16

EPS = 1e-6
GRID_W = 64
ROPE_THETA = 10000.0
HEAD_DIM = 64
HEAD_LANES = 128
N_HEADS = 8
CONV_K = 31
CONV_HALO = 16
SUBLANES = 8
HGRN_HEADS = 16
HGRN_CHUNK = 64
SCAN_BLOCK = 256

KEY_CHUNK = 256
Q_SUB = 256
MAX_FIXED_SHIFT = 40.0
VMEM_LIMIT = 48 * 1024 * 1024


def _silu(x):
    return x * jax.nn.sigmoid(x)


def _params(*sem):
    return pltpu.CompilerParams(dimension_semantics=sem, vmem_limit_bytes=VMEM_LIMIT)


def _ada_kernel(c_ref, w_ref, b_ref, o_ref):
    sc = _silu(c_ref[...])
    o_ref[0] = jnp.dot(sc, w_ref[0], precision=lax.Precision.HIGHEST,
                       preferred_element_type=F32) + b_ref[0]


def _ada(cond_rows, w_ada, b_ada):
    depth, d, n = w_ada.shape
    tn = 768
    return pl.pallas_call(
        _ada_kernel,
        grid=(depth, n // tn),
        in_specs=[pl.BlockSpec((8, d), lambda l, j: (0, 0)),
                  pl.BlockSpec((1, d, tn), lambda l, j: (l, 0, j)),
                  pl.BlockSpec((1, 1, tn), lambda l, j: (l, 0, j))],
        out_specs=pl.BlockSpec((1, 8, tn), lambda l, j: (l, 0, j)),
        out_shape=jax.ShapeDtypeStruct((depth, 8, n), F32),
        compiler_params=_params("arbitrary", "arbitrary"),
        name="ada",
    )(cond_rows, w_ada, b_ada.reshape(depth, 1, n))


def _group_mean_sq(x, gmat):
    sq = x * x
    hi = sq.astype(BF16)
    lo = (sq - hi.astype(F32)).astype(BF16)
    return (jnp.dot(hi, gmat, preferred_element_type=F32)
            + jnp.dot(lo, gmat, preferred_element_type=F32))


def _proj_kernel(*refs, n_qk_tiles, silu_tiles, rope, q_scale, row_chunk, ep_rows):
    if n_qk_tiles:
        (x_ref, g_ref, sc_ref, sh_ref, w_ref, qg_ref, kg_ref, gmat_ref,
         cos_ref, sa_ref, sb_ref, o_ref, vt_ref, xn_ref, acc_ref) = refs
    else:
        x_ref, g_ref, sc_ref, sh_ref, w_ref, o_ref, xn_ref = refs
    j = pl.program_id(1)
    tm, tn = o_ref.shape

    @pl.when(j == 0)
    def _normalise():
        gain = g_ref[...] * (1.0 + sc_ref[0])
        shift = sh_ref[0]

        def body(r, carry):
            r0 = pl.multiple_of(r * row_chunk, row_chunk)
            x = x_ref[pl.ds(r0, row_chunk), :]
            ms = jnp.mean(x * x, axis=-1, keepdims=True)
            xn_ref[pl.ds(r0, row_chunk), :] = (x * lax.rsqrt(ms + EPS) * gain + shift).astype(BF16)
            return carry

        lax.fori_loop(0, tm // row_chunk, body, 0, unroll=2)

    def matmul():
        return jnp.dot(xn_ref[...], w_ref[...], preferred_element_type=F32)

    def norm_rope(gain_ref, scale):
        acc_ref[...] = matmul()
        gmat = gmat_ref[...]
        gain = gain_ref[...]

        def body(r, carry):
            rows = pl.ds(pl.multiple_of(r * ep_rows, ep_rows), ep_rows)
            for c in range(tn // HEAD_LANES):
                cols = slice(c * HEAD_LANES, (c + 1) * HEAD_LANES)
                xs = acc_ref[rows, cols]
                y = xs * lax.rsqrt(_group_mean_sq(xs, gmat) + EPS) * gain
                if rope:
                    y = (y * cos_ref[rows, :]
                         + pltpu.roll(y, HEAD_LANES - 16, axis=1) * sa_ref[rows, :]
                         + pltpu.roll(y, 16, axis=1) * sb_ref[rows, :])
                if scale != 1.0:
                    y = y * scale
                o_ref[rows, cols] = y.astype(BF16)
            return carry

        lax.fori_loop(0, tm // ep_rows, body, 0)

    plain = j >= 2 * n_qk_tiles
    if n_qk_tiles:
        @pl.when(j < n_qk_tiles)
        def _q():
            norm_rope(qg_ref, q_scale)

        @pl.when((j >= n_qk_tiles) & (j < 2 * n_qk_tiles))
        def _k():
            norm_rope(kg_ref, 1.0)

        plain = j > 2 * n_qk_tiles

        @pl.when(j == 2 * n_qk_tiles)
        def _values():
            acc = matmul()
            o_ref[...] = acc.astype(BF16)
            n_seq, n_heads, chunks_per_seq = vt_ref.shape[:3]
            for s in range(n_seq):
                for c in range(chunks_per_seq):
                    r0 = (s * chunks_per_seq + c) * KEY_CHUNK
                    for h in range(n_heads):
                        blk = acc[r0:r0 + KEY_CHUNK, h * HEAD_LANES:(h + 1) * HEAD_LANES]
                        vt_ref[s, h, c] = blk.T.astype(BF16)

    gated = functools.reduce(lambda a, b: a | b, [j == t for t in silu_tiles], j < 0)

    @pl.when(plain & gated)
    def _gated():
        o_ref[...] = _silu(matmul()).astype(BF16)

    @pl.when(plain & jnp.logical_not(gated))
    def _rest():
        o_ref[...] = matmul().astype(BF16)


def _proj(x2d, gain, scale, shift, w_bf, rows_per_group, *, tm, tn, qk=None, silu_cols=(), seq_len=None,
          name):
    m, d = x2d.shape
    n = w_bf.shape[1]
    tiles_per_group = rows_per_group // tm
    in_specs = [
        pl.BlockSpec((tm, d), lambda i, j: (i, 0)),
        pl.BlockSpec((1, d), lambda i, j: (0, 0)),
        pl.BlockSpec((1, 1, d), lambda i, j: (i // tiles_per_group, 0, 0)),
        pl.BlockSpec((1, 1, d), lambda i, j: (i // tiles_per_group, 0, 0)),
        pl.BlockSpec((d, tn), lambda i, j: (0, j)),
    ]
    args = [x2d, gain.reshape(1, d), scale, shift, w_bf]
    scratch = [pltpu.VMEM((tm, d), BF16)]
    out_specs = [pl.BlockSpec((tm, tn), lambda i, j: (i, j))]
    out_shape = [jax.ShapeDtypeStruct((m, n), BF16)]
    n_qk_tiles, rope, q_scale = 0, False, 1.0
    if qk is not None:
        qg, kg, gmat, cos, sa, sb, q_width, q_scale = qk
        n_qk_tiles = q_width // tn
        rope = cos is not None
        const = lambda i, j: (0, 0)
        in_specs += [pl.BlockSpec((1, HEAD_LANES), const), pl.BlockSpec((1, HEAD_LANES), const),
                     pl.BlockSpec((HEAD_LANES, HEAD_LANES), const)]
        args += [qg, kg, gmat]
        if rope:
            pos_tiles = cos.shape[0] // tm
            pos = lambda i, j: (i % pos_tiles, 0)
            in_specs += [pl.BlockSpec((tm, HEAD_LANES), pos)] * 3
            args += [cos, sa, sb]
        else:
            dummy = jnp.zeros((8, HEAD_LANES), F32)
            in_specs += [pl.BlockSpec((8, HEAD_LANES), const)] * 3
            args += [dummy, dummy, dummy]
        scratch.append(pltpu.VMEM((tm, tn), F32))
        assert tn == q_width
        heads = tn // HEAD_LANES
        seq_rows = min(tm, seq_len)
        tiles_per_seq = seq_len // seq_rows
        vt_block = (tm // seq_rows, heads, seq_rows // KEY_CHUNK, HEAD_LANES, KEY_CHUNK)
        out_specs.append(pl.BlockSpec(
            vt_block, lambda i, j: (i // tiles_per_seq, 0, i % tiles_per_seq, 0, 0)))
        out_shape.append(jax.ShapeDtypeStruct(
            (m // seq_len, heads, seq_len // KEY_CHUNK, HEAD_LANES, KEY_CHUNK), BF16))
    silu_tiles = tuple(t for a, b in silu_cols for t in range(a // tn, b // tn))
    kern = functools.partial(_proj_kernel, n_qk_tiles=n_qk_tiles, silu_tiles=silu_tiles,
                             rope=rope, q_scale=q_scale, row_chunk=32, ep_rows=256)
    outs = pl.pallas_call(
        kern,
        grid=(m // tm, n // tn),
        in_specs=in_specs,
        out_specs=out_specs,
        out_shape=out_shape,
        scratch_shapes=scratch,
        compiler_params=_params("arbitrary", "arbitrary"),
        name=name,
    )(*args)
    return outs if qk is not None else outs[0]


def _attn_refs(refs, has_lat):
    if has_lat:
        return refs
    scal_ref, q_ref, kc_ref, vtc_ref, za_ref, sg_ref, o_ref = refs
    return scal_ref, q_ref, kc_ref, vtc_ref, None, None, za_ref, sg_ref, o_ref


def _key_value_chunks(kc_ref, vtc_ref, kl_ref, vtl_ref):
    n_ctx_chunks = vtc_ref.shape[2]
    n_lat_chunks = 0 if vtl_ref is None else vtl_ref.shape[2]

    def ctx(c):
        return kc_ref[0, c * KEY_CHUNK:(c + 1) * KEY_CHUNK, :], vtc_ref[0, 0, c]

    def lat(c):
        return kl_ref[0, pl.ds(pl.multiple_of(c * KEY_CHUNK, KEY_CHUNK), KEY_CHUNK), :], vtl_ref[0, 0, c]

    return n_ctx_chunks, n_lat_chunks, ctx, lat


def _query_maps(q):
    lane = lax.broadcasted_iota(jnp.int32, q.shape, 1)
    zero = jnp.zeros_like(q)
    return jnp.where(lane < HEAD_DIM, q, zero), jnp.where(lane >= HEAD_DIM, q, zero)


def _attn_finish(o_t, za, sg_ref, lam_init):
    o = o_t.T
    ms = jnp.mean(o * o, axis=-1, keepdims=True)
    y = o * lax.rsqrt(ms + EPS) * sg_ref[...] * (1.0 - lam_init)
    return (y * za.astype(F32)).astype(BF16)


def _attn_kernel(*refs, has_lat, lam_init):
    (scal_ref, q_ref, kc_ref, vtc_ref, kl_ref, vtl_ref, za_ref, sg_ref,
     o_ref) = _attn_refs(refs, has_lat)
    tq = q_ref.shape[1]
    n_ctx_chunks, n_lat_chunks, ctx_chunk, lat_chunk = _key_value_chunks(kc_ref, vtc_ref, kl_ref, vtl_ref)
    q_maps = _query_maps(q_ref[0])

    def step(kblk, vtb, carry):
        out = []
        for mp in range(2):
            m_old, l_old, o_old = carry[3 * mp:3 * mp + 3]
            s = lax.dot_general(kblk, q_maps[mp], (((1,), (1,)), ((), ())),
                                preferred_element_type=F32)
            m_new = jnp.maximum(m_old, jnp.max(s, axis=0, keepdims=True))
            p = jnp.exp(s - m_new)
            alpha = jnp.exp(m_old - m_new)
            l_new = alpha * l_old + jnp.sum(p, axis=0, keepdims=True)
            o_new = alpha * o_old + jnp.dot(vtb, p.astype(BF16), preferred_element_type=F32)
            out += [m_new, l_new, o_new]
        return tuple(out)

    m0 = jnp.full((1, tq), -1e30, F32)
    l0 = jnp.zeros((1, tq), F32)
    o0 = jnp.zeros((HEAD_LANES, tq), F32)
    carry = (m0, l0, o0, m0, l0, o0)
    for c in range(n_ctx_chunks):
        carry = step(*ctx_chunk(c), carry)
    if n_lat_chunks:
        carry = lax.fori_loop(0, n_lat_chunks, lambda c, acc: step(*lat_chunk(c), acc), carry)
    _, l_a, o_a, _, l_b, o_b = carry
    o_ref[0] = _attn_finish(o_a / l_a - scal_ref[0] * (o_b / l_b), za_ref[0], sg_ref, lam_init)


def _attn_bounded_kernel(*refs, has_lat, lam_init):
    (scal_ref, q_ref, kc_ref, vtc_ref, kl_ref, vtl_ref, za_ref, sg_ref,
     o_ref) = _attn_refs(refs, has_lat)
    tq = q_ref.shape[1]
    n_ctx_chunks, n_lat_chunks, ctx_chunk, lat_chunk = _key_value_chunks(kc_ref, vtc_ref, kl_ref, vtl_ref)
    n_chunks = n_ctx_chunks + n_lat_chunks

    def chunk(c):
        return ctx_chunk(c) if c < n_ctx_chunks else lat_chunk(c - n_ctx_chunks)

    shift = scal_ref[1]
    steps = [(t, c) for t in range(tq // Q_SUB) for c in range(n_chunks)]
    q_maps = {}

    def scores(t, c):
        if t not in q_maps:
            q_maps[t] = _query_maps(q_ref[0, t * Q_SUB:(t + 1) * Q_SUB, :])
        kblk = chunk(c)[0]
        return [lax.dot_general(kblk, q_maps[t][mp], (((1,), (1,)), ((), ())),
                                preferred_element_type=F32) for mp in range(2)]

    s_next = scores(*steps[0])
    for i, (t, c) in enumerate(steps):
        if c == 0:
            l_acc = [jnp.zeros((SUBLANES, Q_SUB), F32) for _ in range(2)]
            o_acc = [jnp.zeros((HEAD_LANES, Q_SUB), F32) for _ in range(2)]
        s_cur = s_next
        if i + 1 < len(steps):
            s_next = scores(*steps[i + 1])
        vtb = chunk(c)[1]
        for mp in range(2):
            p = jnp.exp(s_cur[mp] - shift)
            l_acc[mp] = l_acc[mp] + jnp.sum(p.reshape(KEY_CHUNK // SUBLANES, SUBLANES, Q_SUB), axis=0)
            o_acc[mp] = o_acc[mp] + jnp.dot(vtb, p.astype(BF16), preferred_element_type=F32)
        if c == n_chunks - 1:
            l_a = jnp.sum(l_acc[0], axis=0, keepdims=True)
            l_b = jnp.sum(l_acc[1], axis=0, keepdims=True)
            rows = slice(t * Q_SUB, (t + 1) * Q_SUB)
            o_ref[0, rows, :] = _attn_finish(o_acc[0] / l_a - scal_ref[0] * (o_acc[1] / l_b),
                                             za_ref[0, rows, :], sg_ref, lam_init)


def _attention(scal, proj_q, proj_ctx, vt_ctx, proj_lat, vt_lat, subln_g, *, tq, att_width, lam_init,
               bounded, name):
    bsz, lq, _ = proj_q.shape
    n_ctx = proj_ctx.shape[1]
    nh = att_width // HEAD_LANES
    k_off, z_off = nh, 3 * nh
    vt_spec = lambda vt: pl.BlockSpec((1, 1) + vt.shape[2:], lambda b, h, i: (b, h, 0, 0, 0))
    in_specs = [
        pl.BlockSpec(memory_space=pltpu.SMEM),
        pl.BlockSpec((1, tq, HEAD_LANES), lambda b, h, i: (b, i, h)),
        pl.BlockSpec((1, n_ctx, HEAD_LANES), lambda b, h, i: (b, 0, k_off + h)),
        vt_spec(vt_ctx),
    ]
    args = [scal, proj_q, proj_ctx, vt_ctx]
    if proj_lat is not None:
        n_lat = proj_lat.shape[1]
        in_specs += [pl.BlockSpec((1, n_lat, HEAD_LANES), lambda b, h, i: (b, 0, k_off + h)),
                     vt_spec(vt_lat)]
        args += [proj_lat, vt_lat]
    in_specs += [pl.BlockSpec((1, tq, HEAD_LANES), lambda b, h, i: (b, i, z_off + h)),
                 pl.BlockSpec((1, HEAD_LANES), lambda b, h, i: (0, 0))]
    args += [proj_q, subln_g.reshape(1, HEAD_LANES)]
    kern = functools.partial(_attn_bounded_kernel if bounded else _attn_kernel,
                             has_lat=proj_lat is not None, lam_init=lam_init)
    return pl.pallas_call(
        kern,
        grid=(bsz, nh, lq // tq),
        in_specs=in_specs,
        out_specs=pl.BlockSpec((1, tq, HEAD_LANES), lambda b, h, i: (b, i, h)),
        out_shape=jax.ShapeDtypeStruct((bsz, lq, att_width), BF16),
        compiler_params=_params("arbitrary", "arbitrary", "arbitrary"),
        name=name,
    )(*args)


def _conv_kernel(gv_ref, gg_ref, zb_ref, gvp_ref, ggp_ref, gvn_ref, ggn_ref,
                 w_ref, b_ref, lg_ref, lb_ref, o_ref, ypad_ref, ysh_ref, *, row_chunk):
    i = pl.program_id(1)
    n_tiles = pl.num_programs(1)
    tl = o_ref.shape[1]
    padded = tl + 2 * CONV_HALO

    def glu(val_ref, gate_ref, rows):
        return val_ref[0, rows, :].astype(F32) * jax.nn.sigmoid(gate_ref[0, rows, :].astype(F32))

    halo = slice(0, CONV_HALO)
    ypad_ref[0:CONV_HALO, :] = jnp.where(i > 0, glu(gvp_ref, ggp_ref, halo), 0.0)
    ypad_ref[CONV_HALO + tl:padded, :] = jnp.where(i < n_tiles - 1, glu(gvn_ref, ggn_ref, halo), 0.0)
    for r in range(tl // row_chunk):
        rows = slice(r * row_chunk, (r + 1) * row_chunk)
        ypad_ref[CONV_HALO + r * row_chunk:CONV_HALO + (r + 1) * row_chunk, :] = glu(gv_ref, gg_ref, rows)

    span = padded - SUBLANES
    for s in range(1, SUBLANES):
        for r in range(0, span, row_chunk):
            n = min(row_chunk, span - r)
            ysh_ref[s, r:r + n, :] = ypad_ref[r + s:r + s + n, :]

    first = CONV_HALO - CONV_K // 2
    for r in range(tl // row_chunk):
        acc = None
        for t in range(CONV_K):
            start = r * row_chunk + first + t
            phase, base = start % SUBLANES, start - start % SUBLANES
            src = ypad_ref[base:base + row_chunk, :] if phase == 0 else ysh_ref[phase, base:base + row_chunk, :]
            term = src.reshape(row_chunk // SUBLANES, SUBLANES, -1) * w_ref[t][None]
            acc = term if acc is None else acc + term
        acc = acc.reshape(row_chunk, -1) + b_ref[...]
        mu = jnp.mean(acc, axis=-1, keepdims=True)
        xc = acc - mu
        var = jnp.mean(xc * xc, axis=-1, keepdims=True)
        y = xc * lax.rsqrt(var + EPS) * lg_ref[...] + lb_ref[...]
        rows = slice(r * row_chunk, (r + 1) * row_chunk)
        o_ref[0, rows, :] = (_silu(y) * zb_ref[0, rows, :].astype(F32)).astype(BF16)


def _conv_branch(proj, conv_w, conv_b, cln_g, cln_b, *, tl, col0, width, name):
    bsz, length, _ = proj.shape
    n_tiles = length // tl
    halo_per_tile = tl // CONV_HALO
    last_halo = length // CONV_HALO - 1
    main = lambda off: pl.BlockSpec((1, tl, width), lambda b, i: (b, i, col0 + off))
    prev = lambda off: pl.BlockSpec(
        (1, CONV_HALO, width), lambda b, i: (b, jnp.maximum(i * halo_per_tile - 1, 0), col0 + off))
    nxt = lambda off: pl.BlockSpec(
        (1, CONV_HALO, width),
        lambda b, i: (b, jnp.minimum((i + 1) * halo_per_tile, last_halo), col0 + off))
    row = lambda k: pl.BlockSpec((k, width), lambda b, i: (0, 0))
    padded = tl + 2 * CONV_HALO
    return pl.pallas_call(
        functools.partial(_conv_kernel, row_chunk=32),
        grid=(bsz, n_tiles),
        in_specs=[main(0), main(1), main(2), prev(0), prev(1), nxt(0), nxt(1),
                  pl.BlockSpec((CONV_K, SUBLANES, width), lambda b, i: (0, 0, 0)),
                  row(1), row(1), row(1)],
        out_specs=pl.BlockSpec((1, tl, width), lambda b, i: (b, i, 0)),
        out_shape=jax.ShapeDtypeStruct((bsz, length, width), BF16),
        scratch_shapes=[pltpu.VMEM((padded, width), F32),
                        pltpu.VMEM((SUBLANES, padded, width), F32)],
        compiler_params=_params("arbitrary", "arbitrary"),
        name=name,
    )(proj, proj, proj, proj, proj, proj, proj,
      jnp.broadcast_to(conv_w[:, None, :], (CONV_K, SUBLANES, width)),
      conv_b.reshape(1, width), cln_g.reshape(1, width), cln_b.reshape(1, width))


def _out_ab_kernel(a_ref, c_ref, wa_ref, wc_ref, x_ref, gate_ref, o_ref):
    y = (jnp.dot(a_ref[...], wa_ref[...], preferred_element_type=F32)
         + jnp.dot(c_ref[...], wc_ref[...], preferred_element_type=F32))
    o_ref[...] = x_ref[...] + gate_ref[0] * y


def _out_ab(att, conv, w_bf, x2d, gate, rows_per_group, *, tm, tn, name):
    m, d = x2d.shape
    ka = att.shape[1]
    kc = conv.shape[1]
    tiles_per_group = rows_per_group // tm
    return pl.pallas_call(
        _out_ab_kernel,
        grid=(m // tm, d // tn),
        in_specs=[pl.BlockSpec((tm, ka), lambda i, j: (i, 0)),
                  pl.BlockSpec((tm, kc), lambda i, j: (i, 0)),
                  pl.BlockSpec((ka, tn), lambda i, j: (0, j)),
                  pl.BlockSpec((kc, tn), lambda i, j: (ka // kc, j)),
                  pl.BlockSpec((tm, tn), lambda i, j: (i, j)),
                  pl.BlockSpec((1, 1, tn), lambda i, j: (i // tiles_per_group, 0, j))],
        out_specs=pl.BlockSpec((tm, tn), lambda i, j: (i, j)),
        out_shape=jax.ShapeDtypeStruct((m, d), F32),
        compiler_params=_params("arbitrary", "arbitrary"),
        name=name,
    )(att, conv, w_bf, w_bf, x2d, gate)


def _out_c_kernel(of_ref, ob_ref, z_ref, g_ref, w_ref, x_ref, gate_ref, o_ref, y_ref, *, row_chunk):
    j = pl.program_id(1)
    tm, d = y_ref.shape

    @pl.when(j == 0)
    def _readout():
        def body(r, carry):
            rows = pl.ds(pl.multiple_of(r * row_chunk, row_chunk), row_chunk)
            for h in range(d // HEAD_LANES):
                cols = slice(h * HEAD_LANES, (h + 1) * HEAD_LANES)
                o = of_ref[rows, cols].astype(F32) + ob_ref[rows, cols].astype(F32)
                ms = jnp.mean(o * o, axis=-1, keepdims=True)
                y = o * lax.rsqrt(ms + EPS) * g_ref[...]
                y_ref[rows, cols] = (y * z_ref[rows, cols].astype(F32)).astype(BF16)
            return carry
        lax.fori_loop(0, tm // row_chunk, body, 0)

    y = jnp.dot(y_ref[...], w_ref[...], preferred_element_type=F32)
    o_ref[...] = x_ref[...] + gate_ref[0] * y


def _out_c(o_f, o_b, proj, z_col, onorm_g, w_bf, x2d, gate, rows_per_group, *, tm, tn, name):
    m, d = x2d.shape
    tiles_per_group = rows_per_group // tm
    return pl.pallas_call(
        functools.partial(_out_c_kernel, row_chunk=64),
        grid=(m // tm, d // tn),
        in_specs=[pl.BlockSpec((tm, d), lambda i, j: (i, 0)),
                  pl.BlockSpec((tm, d), lambda i, j: (i, 0)),
                  pl.BlockSpec((tm, d), lambda i, j: (i, z_col)),
                  pl.BlockSpec((1, HEAD_LANES), lambda i, j: (0, 0)),
                  pl.BlockSpec((d, tn), lambda i, j: (0, j)),
                  pl.BlockSpec((tm, tn), lambda i, j: (i, j)),
                  pl.BlockSpec((1, 1, tn), lambda i, j: (i // tiles_per_group, 0, j))],
        out_specs=pl.BlockSpec((tm, tn), lambda i, j: (i, j)),
        out_shape=jax.ShapeDtypeStruct((m, d), F32),
        scratch_shapes=[pltpu.VMEM((tm, d), BF16)],
        compiler_params=_params("arbitrary", "arbitrary"),
        name=name,
    )(o_f, o_b, proj, onorm_g.reshape(1, HEAD_LANES), w_bf, x2d, gate)


def _chunk_running_sum(x, tri_bf):
    hi = x.astype(BF16)
    lo = (x - hi.astype(F32)).astype(BF16)
    return (jnp.dot(tri_bf, hi, preferred_element_type=F32)
            + jnp.dot(tri_bf, lo, preferred_element_type=F32))


def _scan_kernel(*refs, has_s0, emit_o, emit_s):
    refs = list(refs)
    qf_ref, vf_ref, uf_ref, qb_ref, vb_ref, ub_ref, lb_ref, tri_ref = refs[:8]
    refs = refs[8:]
    s0_ref = refs.pop(0) if has_s0 else None
    of_ref, ob_ref = (refs.pop(0), refs.pop(0)) if emit_o else (None, None)
    sout_ref = refs.pop(0) if emit_s else None
    st_ref, = refs
    n = pl.program_id(2)
    tb = qf_ref.shape[1]

    @pl.when(n == 0)
    def _init():
        if has_s0:
            st_ref[...] = s0_ref[:, 0, 0]
        else:
            st_ref[...] = jnp.zeros_like(st_ref)

    sources = ((qf_ref, vf_ref, uf_ref, of_ref), (qb_ref, vb_ref, ub_ref, ob_ref))
    blk = tri_ref.shape[1]
    n_sub = tb // blk
    n_chunks = blk // HGRN_CHUNK
    work = []
    for d in range(2):
        for sb in (range(n_sub - 1, -1, -1) if d == 1 else range(n_sub)):
            work.append(dict(d=d, rows=slice(sb * blk, (sb + 1) * blk), base=sb * blk))

    for w in work:
        d = w["d"]
        lbd = lb_ref[d]
        sig = jax.nn.sigmoid(sources[d][2][0, w["rows"], :].astype(F32))
        logf = jnp.log(lbd + (1.0 - lbd) * sig)
        w["k"] = (1.0 - lbd) * (1.0 - sig)
        w["b"] = _chunk_running_sum(logf, tri_ref[d].astype(BF16))

    for w in work:
        d = w["d"]
        q_ref, v_ref = sources[d][:2]
        decay = jnp.exp(w["b"])
        k_inv = w["k"] / decay
        v_bf = v_ref[0, w["rows"], :]
        q_dec = (q_ref[0, w["rows"], :].astype(F32) * decay).astype(BF16)
        if emit_o:
            w["a"] = lax.dot_general(q_dec, k_inv.astype(BF16), (((1,), (1,)), ((), ())),
                                     preferred_element_type=F32)
        decay_t = decay.T
        chunks = []
        for c in (range(n_chunks - 1, -1, -1) if d == 1 else range(n_chunks)):
            rows = slice(c * HGRN_CHUNK, (c + 1) * HGRN_CHUNK)
            last = c * HGRN_CHUNK if d == 1 else (c + 1) * HGRN_CHUNK - 1
            k_end = (k_inv[rows] * decay[last:last + 1, :]).astype(BF16)
            kv = lax.dot_general(k_end, v_bf[rows], (((0,), (0,)), ((), ())),
                                 preferred_element_type=F32)
            chunks.append((rows, decay_t[:, last:last + 1], kv))
        w.update(q_dec=q_dec, v_bf=v_bf, chunks=chunks)

    if emit_o:
        for w in work:
            a = jnp.where(tri_ref[w["d"]] > 0.5, w["a"], 0.0).astype(BF16)
            w["o_intra"] = jnp.dot(a, w["v_bf"], preferred_element_type=F32)

    readouts = [[], []]
    for d in range(2):
        st = st_ref[d]
        for w in work:
            if w["d"] != d:
                continue
            for rows, decay_end, kv in w["chunks"]:
                readouts[d].append((w, rows, st.astype(BF16)))
                st = decay_end * st + kv
        st_ref[d] = st

    if emit_o:
        for pair in zip(*readouts):
            for d, (w, rows, st_bf) in enumerate(pair):
                o = w["o_intra"][rows] + jnp.dot(w["q_dec"][rows], st_bf, preferred_element_type=F32)
                sources[d][3][0, w["base"] + rows.start:w["base"] + rows.stop, :] = o.astype(BF16)

    if emit_s:
        @pl.when(n == pl.num_programs(2) - 1)
        def _final():
            sout_ref[:, 0, 0] = st_ref[...]


def _scan(proj, lb, s0, *, tb, n_heads, emit_o, emit_s, name):
    bsz, length, _ = proj.shape
    nb = length // tb
    hd = HEAD_LANES
    fwd = lambda off: pl.BlockSpec((1, tb, hd), lambda b, h, n: (b, n, off * n_heads + h))
    bwd = lambda off: pl.BlockSpec((1, tb, hd), lambda b, h, n: (b, nb - 1 - n, off * n_heads + h))
    state = pl.BlockSpec((2, 1, 1, hd, hd), lambda b, h, n: (0, b, h, 0, 0))
    blk = min(tb, SCAN_BLOCK)
    tok = jnp.arange(blk)
    same_chunk = (tok[:, None] // HGRN_CHUNK) == (tok[None, :] // HGRN_CHUNK)
    tri = jnp.stack([same_chunk & (tok[None, :] <= tok[:, None]),
                     same_chunk & (tok[None, :] >= tok[:, None])]).astype(F32)
    in_specs = [fwd(0), fwd(1), fwd(2), bwd(0), bwd(1), bwd(3),
                pl.BlockSpec((2, 1, hd), lambda b, h, n: (0, 0, h)),
                pl.BlockSpec((2, blk, blk), lambda b, h, n: (0, 0, 0))]
    args = [proj] * 6 + [lb, tri]
    if s0 is not None:
        in_specs.append(state)
        args.append(s0)
    out_specs, out_shape = [], []
    if emit_o:
        out_specs += [pl.BlockSpec((1, tb, hd), lambda b, h, n: (b, n, h)),
                      pl.BlockSpec((1, tb, hd), lambda b, h, n: (b, nb - 1 - n, h))]
        out_shape += [jax.ShapeDtypeStruct((bsz, length, n_heads * hd), BF16)] * 2
    if emit_s:
        out_specs.append(state)
        out_shape.append(jax.ShapeDtypeStruct((2, bsz, n_heads, hd, hd), F32))
    kern = functools.partial(_scan_kernel, has_s0=s0 is not None, emit_o=emit_o, emit_s=emit_s)
    return pl.pallas_call(
        kern,
        grid=(bsz, n_heads, nb),
        in_specs=in_specs,
        out_specs=out_specs,
        out_shape=out_shape,
        scratch_shapes=[pltpu.VMEM((2, hd, hd), F32)],
        compiler_params=_params("arbitrary", "arbitrary", "arbitrary"),
        name=name,
    )(*args)


def _rope_tables(length):
    half = HEAD_DIM // 2
    rows = length // GRID_W
    row = jnp.repeat(jnp.arange(rows), GRID_W).astype(F32)
    col = jnp.tile(jnp.arange(GRID_W), rows).astype(F32)
    inv = ROPE_THETA ** (-jnp.arange(0, half, 2, dtype=F32) / half)
    def axis_angles(pos):
        a = pos[:, None] * inv[None, :]
        return jnp.concatenate([a, a], axis=-1)
    ang = jnp.concatenate([axis_angles(row), axis_angles(col)], axis=-1)
    ang = jnp.concatenate([ang, ang], axis=-1)
    first_half = (jnp.arange(HEAD_LANES) % half) < half // 2
    sin = jnp.sin(ang)
    return jnp.cos(ang), jnp.where(first_half, -sin, 0.0), jnp.where(first_half, 0.0, sin)


def kernel(x, c, ctx, c_ctx, w_ada, b_ada, norm_g, w_in_ab, w_out_ab, qn_g, kn_g, lam_q1, lam_k1,
           lam_q2, lam_k2, subln_g, conv_w, conv_b, cln_g, cln_b, w_in_c, w_out_c, lb_gamma, onorm_g):
    bsz, seq, d = x.shape
    n_ctx = ctx.shape[1]
    depth = w_ada.shape[0]
    assert depth == 2 and w_in_ab.shape[0] == 1 and w_in_c.shape[0] == 1
    att_width = N_HEADS * HEAD_LANES
    conv_width = conv_w.shape[-1]
    assert w_in_ab.shape[-1] == 4 * att_width + 3 * conv_width and conv_width == att_width

    cond_rows = jnp.concatenate([c, c_ctx[None, :], jnp.zeros((8 - bsz - 1, d), F32)], axis=0)
    mod = _ada(cond_rows, w_ada, b_ada)
    def mods(l, rows):
        m = mod[l, rows][:, None, :]
        return m[..., :d], m[..., d:2 * d], m[..., 2 * d:]

    cos, sin_a, sin_b = _rope_tables(seq)
    gmat = jnp.kron(jnp.eye(2, dtype=F32), jnp.full((HEAD_DIM, HEAD_DIM), 1.0 / HEAD_DIM, F32)
                    ).astype(BF16)
    two = lambda g: jnp.concatenate([g, g]).reshape(1, HEAD_LANES)
    q_scale = HEAD_DIM ** -0.5

    x2d = x.reshape(bsz * seq, d)
    ctx2d = ctx.reshape(bsz * n_ctx, d)
    lat_rows = slice(0, bsz)
    ctx_rows = slice(bsz, bsz + 1)

    lam_init = 0.8 - 0.6 * math.exp(-0.3 * 0)
    lam = (jnp.exp(jnp.sum(lam_q1[0] * lam_k1[0])) - jnp.exp(jnp.sum(lam_q2[0] * lam_k2[0]))
           + lam_init).reshape(1)
    shift, scale, gate = mods(0, lat_rows)
    shift_c, scale_c, gate_c = mods(0, ctx_rows)
    w_in = w_in_ab[0].astype(BF16)
    w_out = w_out_ab[0].astype(BF16)
    ab_in = w_in.shape[1]
    qk_lat = (two(qn_g[0]), two(kn_g[0]), gmat, cos, sin_a, sin_b, att_width, q_scale)
    qk_ctx = (two(qn_g[0]), two(kn_g[0]), gmat, None, None, None, att_width, q_scale)
    gates = ((3 * att_width, 4 * att_width), (ab_in - conv_width, ab_in))
    p_lat, vt_lat = _proj(x2d, norm_g[0], scale, shift, w_in, seq, tm=1024, tn=1024, qk=qk_lat,
                          silu_cols=gates, seq_len=seq, name="proj_ab_lat")
    p_ctx, vt_ctx = _proj(ctx2d, norm_g[0], scale_c, shift_c, w_in, bsz * n_ctx, tm=1024, tn=1024,
                          qk=qk_ctx, silu_cols=gates, seq_len=n_ctx, name="proj_ab_ctx")
    p_lat = p_lat.reshape(bsz, seq, ab_in)
    p_ctx = p_ctx.reshape(bsz, n_ctx, ab_in)
    bound = math.sqrt(HEAD_DIM) * jnp.max(jnp.abs(qn_g[0])) * jnp.max(jnp.abs(kn_g[0]))
    scal = jnp.concatenate([lam, bound.reshape(1)])
    def attend(bounded):
        def run():
            lat = _attention(scal, p_lat, p_ctx, vt_ctx, p_lat, vt_lat, subln_g[0], tq=2048,
                             att_width=att_width, lam_init=lam_init, bounded=bounded,
                             name="attn_lat_bounded" if bounded else "attn_lat")
            ctx_ = _attention(scal, p_ctx, p_ctx, vt_ctx, None, None, subln_g[0], tq=n_ctx,
                              att_width=att_width, lam_init=lam_init, bounded=bounded,
                              name="attn_ctx_bounded" if bounded else "attn_ctx")
            return lat, ctx_
        return run
    att_lat, att_ctx = lax.cond(bound <= MAX_FIXED_SHIFT, attend(True), attend(False))
    conv_col0 = 4 * att_width // conv_width
    conv_lat = _conv_branch(p_lat, conv_w[0], conv_b[0], cln_g[0], cln_b[0], tl=256,
                            col0=conv_col0, width=conv_width, name="conv_lat")
    conv_ctx = _conv_branch(p_ctx, conv_w[0], conv_b[0], cln_g[0], cln_b[0], tl=256,
                            col0=conv_col0, width=conv_width, name="conv_ctx")
    x2d = _out_ab(att_lat.reshape(bsz * seq, att_width), conv_lat.reshape(bsz * seq, conv_width),
                  w_out, x2d, gate, seq, tm=512, tn=2048, name="out_ab_lat")
    ctx2d = _out_ab(att_ctx.reshape(bsz * n_ctx, att_width),
                    conv_ctx.reshape(bsz * n_ctx, conv_width),
                    w_out, ctx2d, gate_c, bsz * n_ctx, tm=512, tn=1024, name="out_ab_ctx")

    p = jax.nn.softmax(lb_gamma.astype(F32), axis=1)
    lb_all = jnp.cumsum(p, axis=1) - p[:, :1]
    lb = lb_all[:, 1].reshape(2, 1, d)
    shift, scale, gate = mods(1, lat_rows)
    shift_c, scale_c, _ = mods(1, ctx_rows)
    w_in = w_in_c[0].astype(BF16)
    w_out = w_out_c[0].astype(BF16)
    c_in = w_in.shape[1]
    gates = ((0, d), (4 * d, 5 * d))
    p_lat = _proj(x2d, norm_g[1], scale, shift, w_in, seq, tm=1024, tn=1024, silu_cols=gates,
                  name="proj_c_lat").reshape(bsz, seq, c_in)
    p_ctx = _proj(ctx2d, norm_g[1], scale_c, shift_c, w_in, bsz * n_ctx, tm=1024, tn=1024,
                  silu_cols=gates, name="proj_c_ctx").reshape(bsz, n_ctx, c_in)
    s_ctx, = _scan(p_ctx, lb, None, tb=n_ctx, n_heads=HGRN_HEADS, emit_o=False, emit_s=True,
                   name="scan_ctx")
    o_f, o_b = _scan(p_lat, lb, s_ctx, tb=2048, n_heads=HGRN_HEADS, emit_o=True, emit_s=False,
                     name="scan_lat")
    out = _out_c(o_f.reshape(bsz * seq, d), o_b.reshape(bsz * seq, d),
                 p_lat.reshape(bsz * seq, c_in), 4, onorm_g[0], w_out, x2d, gate, seq,
                 tm=512, tn=2048, name="out_c_lat")
    return out.reshape(bsz, seq, d)
```

```python
import functools
import math

import jax
import jax.numpy as jnp
from jax import lax
from jax.experimental import pallas as pl
from jax.experimental.pallas import tpu as pltpu

F32 = jnp.float32
BF16 = jnp.bfloat16

EPS = 1e-6
GRID_W = 64
ROPE_THETA = 10000.0
HEAD_DIM = 64
HEAD_LANES = 128
N_HEADS = 8
CONV_K = 31
CONV_HALO = 16
SUBLANES = 8
HGRN_HEADS = 16
HGRN_CHUNK = 64
SCAN_BLOCK = 256

KEY_CHUNK = 256
Q_SUB = 256
MAX_FIXED_SHIFT = 40.0
VMEM_LIMIT = 48 * 1024 * 1024


def _silu(x):
    return x * jax.nn.sigmoid(x)


def _params(*sem):
    return pltpu.CompilerParams(dimension_semantics=sem, vmem_limit_bytes=VMEM_LIMIT)


def _ada_kernel(c_ref, w_ref, b_ref, o_ref):
    sc = _silu(c_ref[...])
    o_ref[0] = jnp.dot(sc, w_ref[0], precision=lax.Precision.HIGHEST,
                       preferred_element_type=F32) + b_ref[0]


def _ada(cond_rows, w_ada, b_ada):
    depth, d, n = w_ada.shape
    tn = 768
    return pl.pallas_call(
        _ada_kernel,
        grid=(depth, n // tn),
        in_specs=[pl.BlockSpec((8, d), lambda l, j: (0, 0)),
                  pl.BlockSpec((1, d, tn), lambda l, j: (l, 0, j)),
                  pl.BlockSpec((1, 1, tn), lambda l, j: (l, 0, j))],
        out_specs=pl.BlockSpec((1, 8, tn), lambda l, j: (l, 0, j)),
        out_shape=jax.ShapeDtypeStruct((depth, 8, n), F32),
        compiler_params=_params("arbitrary", "arbitrary"),
        name="ada",
    )(cond_rows, w_ada, b_ada.reshape(depth, 1, n))


def _group_mean_sq(x, gmat):
    sq = x * x
    hi = sq.astype(BF16)
    lo = (sq - hi.astype(F32)).astype(BF16)
    return (jnp.dot(hi, gmat, preferred_element_type=F32)
            + jnp.dot(lo, gmat, preferred_element_type=F32))


def _proj_kernel(*refs, n_qk_tiles, silu_tiles, rope, q_scale, row_chunk, ep_rows):
    if n_qk_tiles:
        (x_ref, g_ref, sc_ref, sh_ref, w_ref, qg_ref, kg_ref, gmat_ref,
         cos_ref, sa_ref, sb_ref, o_ref, vt_ref, xn_ref, acc_ref) = refs
    else:
        x_ref, g_ref, sc_ref, sh_ref, w_ref, o_ref, xn_ref = refs
    j = pl.program_id(1)
    tm, tn = o_ref.shape

    @pl.when(j == 0)
    def _normalise():
        gain = g_ref[...] * (1.0 + sc_ref[0])
        shift = sh_ref[0]

        def body(r, carry):
            r0 = pl.multiple_of(r * row_chunk, row_chunk)
            x = x_ref[pl.ds(r0, row_chunk), :]
            ms = jnp.mean(x * x, axis=-1, keepdims=True)
            xn_ref[pl.ds(r0, row_chunk), :] = (x * lax.rsqrt(ms + EPS) * gain + shift).astype(BF16)
            return carry

        lax.fori_loop(0, tm // row_chunk, body, 0, unroll=2)

    def matmul():
        return jnp.dot(xn_ref[...], w_ref[...], preferred_element_type=F32)

    def norm_rope(gain_ref, scale):
        acc_ref[...] = matmul()
        gmat = gmat_ref[...]
        gain = gain_ref[...]

        def body(r, carry):
            rows = pl.ds(pl.multiple_of(r * ep_rows, ep_rows), ep_rows)
            for c in range(tn // HEAD_LANES):
                cols = slice(c * HEAD_LANES, (c + 1) * HEAD_LANES)
                xs = acc_ref[rows, cols]
                y = xs * lax.rsqrt(_group_mean_sq(xs, gmat) + EPS) * gain
                if rope:
                    y = (y * cos_ref[rows, :]
                         + pltpu.roll(y, HEAD_LANES - 16, axis=1) * sa_ref[rows, :]
                         + pltpu.roll(y, 16, axis=1) * sb_ref[rows, :])
                if scale != 1.0:
                    y = y * scale
                o_ref[rows, cols] = y.astype(BF16)
            return carry

        lax.fori_loop(0, tm // ep_rows, body, 0)

    plain = j >= 2 * n_qk_tiles
    if n_qk_tiles:
        @pl.when(j < n_qk_tiles)
        def _q():
            norm_rope(qg_ref, q_scale)

        @pl.when((j >= n_qk_tiles) & (j < 2 * n_qk_tiles))
        def _k():
            norm_rope(kg_ref, 1.0)

        plain = j > 2 * n_qk_tiles

        @pl.when(j == 2 * n_qk_tiles)
        def _values():
            acc = matmul()
            o_ref[...] = acc.astype(BF16)
            n_seq, n_heads, chunks_per_seq = vt_ref.shape[:3]
            for s in range(n_seq):
                for c in range(chunks_per_seq):
                    r0 = (s * chunks_per_seq + c) * KEY_CHUNK
                    for h in range(n_heads):
                        blk = acc[r0:r0 + KEY_CHUNK, h * HEAD_LANES:(h + 1) * HEAD_LANES]
                        vt_ref[s, h, c] = blk.T.astype(BF16)

    gated = functools.reduce(lambda a, b: a | b, [j == t for t in silu_tiles], j < 0)

    @pl.when(plain & gated)
    def _gated():
        o_ref[...] = _silu(matmul()).astype(BF16)

    @pl.when(plain & jnp.logical_not(gated))
    def _rest():
        o_ref[...] = matmul().astype(BF16)


def _proj(x2d, gain, scale, shift, w_bf, rows_per_group, *, tm, tn, qk=None, silu_cols=(), seq_len=None,
          name):
    m, d = x2d.shape
    n = w_bf.shape[1]
    tiles_per_group = rows_per_group // tm
    in_specs = [
        pl.BlockSpec((tm, d), lambda i, j: (i, 0)),
        pl.BlockSpec((1, d), lambda i, j: (0, 0)),
        pl.BlockSpec((1, 1, d), lambda i, j: (i // tiles_per_group, 0, 0)),
        pl.BlockSpec((1, 1, d), lambda i, j: (i // tiles_per_group, 0, 0)),
        pl.BlockSpec((d, tn), lambda i, j: (0, j)),
    ]
    args = [x2d, gain.reshape(1, d), scale, shift, w_bf]
    scratch = [pltpu.VMEM((tm, d), BF16)]
    out_specs = [pl.BlockSpec((tm, tn), lambda i, j: (i, j))]
    out_shape = [jax.ShapeDtypeStruct((m, n), BF16)]
    n_qk_tiles, rope, q_scale = 0, False, 1.0
    if qk is not None:
        qg, kg, gmat, cos, sa, sb, q_width, q_scale = qk
        n_qk_tiles = q_width // tn
        rope = cos is not None
        const = lambda i, j: (0, 0)
        in_specs += [pl.BlockSpec((1, HEAD_LANES), const), pl.BlockSpec((1, HEAD_LANES), const),
                     pl.BlockSpec((HEAD_LANES, HEAD_LANES), const)]
        args += [qg, kg, gmat]
        if rope:
            pos_tiles = cos.shape[0] // tm
            pos = lambda i, j: (i % pos_tiles, 0)
            in_specs += [pl.BlockSpec((tm, HEAD_LANES), pos)] * 3
            args += [cos, sa, sb]
        else:
            dummy = jnp.zeros((8, HEAD_LANES), F32)
            in_specs += [pl.BlockSpec((8, HEAD_LANES), const)] * 3
            args += [dummy, dummy, dummy]
        scratch.append(pltpu.VMEM((tm, tn), F32))
        assert tn == q_width
        heads = tn // HEAD_LANES
        seq_rows = min(tm, seq_len)
        tiles_per_seq = seq_len // seq_rows
        vt_block = (tm // seq_rows, heads, seq_rows // KEY_CHUNK, HEAD_LANES, KEY_CHUNK)
        out_specs.append(pl.BlockSpec(
            vt_block, lambda i, j: (i // tiles_per_seq, 0, i % tiles_per_seq, 0, 0)))
        out_shape.append(jax.ShapeDtypeStruct(
            (m // seq_len, heads, seq_len // KEY_CHUNK, HEAD_LANES, KEY_CHUNK), BF16))
    silu_tiles = tuple(t for a, b in silu_cols for t in range(a // tn, b // tn))
    kern = functools.partial(_proj_kernel, n_qk_tiles=n_qk_tiles, silu_tiles=silu_tiles,
                             rope=rope, q_scale=q_scale, row_chunk=32, ep_rows=256)
    outs = pl.pallas_call(
        kern,
        grid=(m // tm, n // tn),
        in_specs=in_specs,
        out_specs=out_specs,
        out_shape=out_shape,
        scratch_shapes=scratch,
        compiler_params=_params("arbitrary", "arbitrary"),
        name=name,
    )(*args)
    return outs if qk is not None else outs[0]


def _attn_refs(refs, has_lat):
    if has_lat:
        return refs
    scal_ref, q_ref, kc_ref, vtc_ref, za_ref, sg_ref, o_ref = refs
    return scal_ref, q_ref, kc_ref, vtc_ref, None, None, za_ref, sg_ref, o_ref


def _key_value_chunks(kc_ref, vtc_ref, kl_ref, vtl_ref):
    n_ctx_chunks = vtc_ref.shape[2]
    n_lat_chunks = 0 if vtl_ref is None else vtl_ref.shape[2]

    def ctx(c):
        return kc_ref[0, c * KEY_CHUNK:(c + 1) * KEY_CHUNK, :], vtc_ref[0, 0, c]

    def lat(c):
        return kl_ref[0, pl.ds(pl.multiple_of(c * KEY_CHUNK, KEY_CHUNK), KEY_CHUNK), :], vtl_ref[0, 0, c]

    return n_ctx_chunks, n_lat_chunks, ctx, lat


def _query_maps(q):
    lane = lax.broadcasted_iota(jnp.int32, q.shape, 1)
    zero = jnp.zeros_like(q)
    return jnp.where(lane < HEAD_DIM, q, zero), jnp.where(lane >= HEAD_DIM, q, zero)


def _attn_finish(o_t, za, sg_ref, lam_init):
    o = o_t.T
    ms = jnp.mean(o * o, axis=-1, keepdims=True)
    y = o * lax.rsqrt(ms + EPS) * sg_ref[...] * (1.0 - lam_init)
    return (y * za.astype(F32)).astype(BF16)


def _attn_kernel(*refs, has_lat, lam_init):
    (scal_ref, q_ref, kc_ref, vtc_ref, kl_ref, vtl_ref, za_ref, sg_ref,
     o_ref) = _attn_refs(refs, has_lat)
    tq = q_ref.shape[1]
    n_ctx_chunks, n_lat_chunks, ctx_chunk, lat_chunk = _key_value_chunks(kc_ref, vtc_ref, kl_ref, vtl_ref)
    q_maps = _query_maps(q_ref[0])

    def step(kblk, vtb, carry):
        out = []
        for mp in range(2):
            m_old, l_old, o_old = carry[3 * mp:3 * mp + 3]
            s = lax.dot_general(kblk, q_maps[mp], (((1,), (1,)), ((), ())),
                                preferred_element_type=F32)
            m_new = jnp.maximum(m_old, jnp.max(s, axis=0, keepdims=True))
            p = jnp.exp(s - m_new)
            alpha = jnp.exp(m_old - m_new)
            l_new = alpha * l_old + jnp.sum(p, axis=0, keepdims=True)
            o_new = alpha * o_old + jnp.dot(vtb, p.astype(BF16), preferred_element_type=F32)
            out += [m_new, l_new, o_new]
        return tuple(out)

    m0 = jnp.full((1, tq), -1e30, F32)
    l0 = jnp.zeros((1, tq), F32)
    o0 = jnp.zeros((HEAD_LANES, tq), F32)
    carry = (m0, l0, o0, m0, l0, o0)
    for c in range(n_ctx_chunks):
        carry = step(*ctx_chunk(c), carry)
    if n_lat_chunks:
        carry = lax.fori_loop(0, n_lat_chunks, lambda c, acc: step(*lat_chunk(c), acc), carry)
    _, l_a, o_a, _, l_b, o_b = carry
    o_ref[0] = _attn_finish(o_a / l_a - scal_ref[0] * (o_b / l_b), za_ref[0], sg_ref, lam_init)


def _attn_bounded_kernel(*refs, has_lat, lam_init):
    (scal_ref, q_ref, kc_ref, vtc_ref, kl_ref, vtl_ref, za_ref, sg_ref,
     o_ref) = _attn_refs(refs, has_lat)
    tq = q_ref.shape[1]
    n_ctx_chunks, n_lat_chunks, ctx_chunk, lat_chunk = _key_value_chunks(kc_ref, vtc_ref, kl_ref, vtl_ref)
    n_chunks = n_ctx_chunks + n_lat_chunks

    def chunk(c):
        return ctx_chunk(c) if c < n_ctx_chunks else lat_chunk(c - n_ctx_chunks)

    shift = scal_ref[1]
    steps = [(t, c) for t in range(tq // Q_SUB) for c in range(n_chunks)]
    q_maps = {}

    def scores(t, c):
        if t not in q_maps:
            q_maps[t] = _query_maps(q_ref[0, t * Q_SUB:(t + 1) * Q_SUB, :])
        kblk = chunk(c)[0]
        return [lax.dot_general(kblk, q_maps[t][mp], (((1,), (1,)), ((), ())),
                                preferred_element_type=F32) for mp in range(2)]

    s_next = scores(*steps[0])
    for i, (t, c) in enumerate(steps):
        if c == 0:
            l_acc = [jnp.zeros((SUBLANES, Q_SUB), F32) for _ in range(2)]
            o_acc = [jnp.zeros((HEAD_LANES, Q_SUB), F32) for _ in range(2)]
        s_cur = s_next
        if i + 1 < len(steps):
            s_next = scores(*steps[i + 1])
        vtb = chunk(c)[1]
        for mp in range(2):
            p = jnp.exp((s_cur[mp] - shift).astype(BF16))
            l_acc[mp] = l_acc[mp] + jnp.sum(
                p.astype(F32).reshape(KEY_CHUNK // SUBLANES, SUBLANES, Q_SUB), axis=0)
            o_acc[mp] = o_acc[mp] + jnp.dot(vtb, p, preferred_element_type=F32)
        if c == n_chunks - 1:
            l_a = jnp.sum(l_acc[0], axis=0, keepdims=True)
            l_b = jnp.sum(l_acc[1], axis=0, keepdims=True)
            rows = slice(t * Q_SUB, (t + 1) * Q_SUB)
            o_ref[0, rows, :] = _attn_finish(o_acc[0] / l_a - scal_ref[0] * (o_acc[1] / l_b),
                                             za_ref[0, rows, :], sg_ref, lam_init)


def _attention(scal, proj_q, proj_ctx, vt_ctx, proj_lat, vt_lat, subln_g, *, tq, att_width, lam_init,
               bounded, name):
    bsz, lq, _ = proj_q.shape
    n_ctx = proj_ctx.shape[1]
    nh = att_width // HEAD_LANES
    k_off, z_off = nh, 3 * nh
    vt_spec = lambda vt: pl.BlockSpec((1, 1) + vt.shape[2:], lambda b, h, i: (b, h, 0, 0, 0))
    in_specs = [
        pl.BlockSpec(memory_space=pltpu.SMEM),
        pl.BlockSpec((1, tq, HEAD_LANES), lambda b, h, i: (b, i, h)),
        pl.BlockSpec((1, n_ctx, HEAD_LANES), lambda b, h, i: (b, 0, k_off + h)),
        vt_spec(vt_ctx),
    ]
    args = [scal, proj_q, proj_ctx, vt_ctx]
    if proj_lat is not None:
        n_lat = proj_lat.shape[1]
        in_specs += [pl.BlockSpec((1, n_lat, HEAD_LANES), lambda b, h, i: (b, 0, k_off + h)),
                     vt_spec(vt_lat)]
        args += [proj_lat, vt_lat]
    in_specs += [pl.BlockSpec((1, tq, HEAD_LANES), lambda b, h, i: (b, i, z_off + h)),
                 pl.BlockSpec((1, HEAD_LANES), lambda b, h, i: (0, 0))]
    args += [proj_q, subln_g.reshape(1, HEAD_LANES)]
    kern = functools.partial(_attn_bounded_kernel if bounded else _attn_kernel,
                             has_lat=proj_lat is not None, lam_init=lam_init)
    return pl.pallas_call(
        kern,
        grid=(bsz, nh, lq // tq),
        in_specs=in_specs,
        out_specs=pl.BlockSpec((1, tq, HEAD_LANES), lambda b, h, i: (b, i, h)),
        out_shape=jax.ShapeDtypeStruct((bsz, lq, att_width), BF16),
        compiler_params=_params("arbitrary", "arbitrary", "arbitrary"),
        name=name,
    )(*args)


def _conv_kernel(gv_ref, gg_ref, zb_ref, gvp_ref, ggp_ref, gvn_ref, ggn_ref,
                 w_ref, b_ref, lg_ref, lb_ref, o_ref, ypad_ref, ysh_ref, *, row_chunk):
    i = pl.program_id(1)
    n_tiles = pl.num_programs(1)
    tl = o_ref.shape[1]
    padded = tl + 2 * CONV_HALO

    def glu(val_ref, gate_ref, rows):
        return val_ref[0, rows, :].astype(F32) * jax.nn.sigmoid(gate_ref[0, rows, :].astype(F32))

    halo = slice(0, CONV_HALO)
    ypad_ref[0:CONV_HALO, :] = jnp.where(i > 0, glu(gvp_ref, ggp_ref, halo), 0.0)
    ypad_ref[CONV_HALO + tl:padded, :] = jnp.where(i < n_tiles - 1, glu(gvn_ref, ggn_ref, halo), 0.0)
    for r in range(tl // row_chunk):
        rows = slice(r * row_chunk, (r + 1) * row_chunk)
        ypad_ref[CONV_HALO + r * row_chunk:CONV_HALO + (r + 1) * row_chunk, :] = glu(gv_ref, gg_ref, rows)

    span = padded - SUBLANES
    for s in range(1, SUBLANES):
        for r in range(0, span, row_chunk):
            n = min(row_chunk, span - r)
            ysh_ref[s, r:r + n, :] = ypad_ref[r + s:r + s + n, :]

    first = CONV_HALO - CONV_K // 2
    for r in range(tl // row_chunk):
        acc = None
        for t in range(CONV_K):
            start = r * row_chunk + first + t
            phase, base = start % SUBLANES, start - start % SUBLANES
            src = ypad_ref[base:base + row_chunk, :] if phase == 0 else ysh_ref[phase, base:base + row_chunk, :]
            term = src.reshape(row_chunk // SUBLANES, SUBLANES, -1) * w_ref[t][None]
            acc = term if acc is None else acc + term
        acc = acc.reshape(row_chunk, -1) + b_ref[...]
        mu = jnp.mean(acc, axis=-1, keepdims=True)
        xc = acc - mu
        var = jnp.mean(xc * xc, axis=-1, keepdims=True)
        y = xc * lax.rsqrt(var + EPS) * lg_ref[...] + lb_ref[...]
        rows = slice(r * row_chunk, (r + 1) * row_chunk)
        o_ref[0, rows, :] = (_silu(y) * zb_ref[0, rows, :].astype(F32)).astype(BF16)


def _conv_branch(proj, conv_w, conv_b, cln_g, cln_b, *, tl, col0, width, name):
    bsz, length, _ = proj.shape
    n_tiles = length // tl
    halo_per_tile = tl // CONV_HALO
    last_halo = length // CONV_HALO - 1
    main = lambda off: pl.BlockSpec((1, tl, width), lambda b, i: (b, i, col0 + off))
    prev = lambda off: pl.BlockSpec(
        (1, CONV_HALO, width), lambda b, i: (b, jnp.maximum(i * halo_per_tile - 1, 0), col0 + off))
    nxt = lambda off: pl.BlockSpec(
        (1, CONV_HALO, width),
        lambda b, i: (b, jnp.minimum((i + 1) * halo_per_tile, last_halo), col0 + off))
    row = lambda k: pl.BlockSpec((k, width), lambda b, i: (0, 0))
    padded = tl + 2 * CONV_HALO
    return pl.pallas_call(
        functools.partial(_conv_kernel, row_chunk=32),
        grid=(bsz, n_tiles),
        in_specs=[main(0), main(1), main(2), prev(0), prev(1), nxt(0), nxt(1),
                  pl.BlockSpec((CONV_K, SUBLANES, width), lambda b, i: (0, 0, 0)),
                  row(1), row(1), row(1)],
        out_specs=pl.BlockSpec((1, tl, width), lambda b, i: (b, i, 0)),
        out_shape=jax.ShapeDtypeStruct((bsz, length, width), BF16),
        scratch_shapes=[pltpu.VMEM((padded, width), F32),
                        pltpu.VMEM((SUBLANES, padded, width), F32)],
        compiler_params=_params("arbitrary", "arbitrary"),
        name=name,
    )(proj, proj, proj, proj, proj, proj, proj,
      jnp.broadcast_to(conv_w[:, None, :], (CONV_K, SUBLANES, width)),
      conv_b.reshape(1, width), cln_g.reshape(1, width), cln_b.reshape(1, width))


def _out_ab_kernel(a_ref, c_ref, wa_ref, wc_ref, x_ref, gate_ref, o_ref):
    y = (jnp.dot(a_ref[...], wa_ref[...], preferred_element_type=F32)
         + jnp.dot(c_ref[...], wc_ref[...], preferred_element_type=F32))
    o_ref[...] = x_ref[...] + gate_ref[0] * y


def _out_ab(att, conv, w_bf, x2d, gate, rows_per_group, *, tm, tn, name):
    m, d = x2d.shape
    ka = att.shape[1]
    kc = conv.shape[1]
    tiles_per_group = rows_per_group // tm
    return pl.pallas_call(
        _out_ab_kernel,
        grid=(m // tm, d // tn),
        in_specs=[pl.BlockSpec((tm, ka), lambda i, j: (i, 0)),
                  pl.BlockSpec((tm, kc), lambda i, j: (i, 0)),
                  pl.BlockSpec((ka, tn), lambda i, j: (0, j)),
                  pl.BlockSpec((kc, tn), lambda i, j: (ka // kc, j)),
                  pl.BlockSpec((tm, tn), lambda i, j: (i, j)),
                  pl.BlockSpec((1, 1, tn), lambda i, j: (i // tiles_per_group, 0, j))],
        out_specs=pl.BlockSpec((tm, tn), lambda i, j: (i, j)),
        out_shape=jax.ShapeDtypeStruct((m, d), F32),
        compiler_params=_params("arbitrary", "arbitrary"),
        name=name,
    )(att, conv, w_bf, w_bf, x2d, gate)


def _out_c_kernel(of_ref, ob_ref, z_ref, g_ref, w_ref, x_ref, gate_ref, o_ref, y_ref, *, row_chunk):
    j = pl.program_id(1)
    tm, d = y_ref.shape

    @pl.when(j == 0)
    def _readout():
        def body(r, carry):
            rows = pl.ds(pl.multiple_of(r * row_chunk, row_chunk), row_chunk)
            for h in range(d // HEAD_LANES):
                cols = slice(h * HEAD_LANES, (h + 1) * HEAD_LANES)
                o = of_ref[rows, cols].astype(F32) + ob_ref[rows, cols].astype(F32)
                ms = jnp.mean(o * o, axis=-1, keepdims=True)
                y = o * lax.rsqrt(ms + EPS) * g_ref[...]
                y_ref[rows, cols] = (y * z_ref[rows, cols].astype(F32)).astype(BF16)
            return carry
        lax.fori_loop(0, tm // row_chunk, body, 0)

    y = jnp.dot(y_ref[...], w_ref[...], preferred_element_type=F32)
    o_ref[...] = x_ref[...] + gate_ref[0] * y


def _out_c(o_f, o_b, proj, z_col, onorm_g, w_bf, x2d, gate, rows_per_group, *, tm, tn, name):
    m, d = x2d.shape
    tiles_per_group = rows_per_group // tm
    return pl.pallas_call(
        functools.partial(_out_c_kernel, row_chunk=64),
        grid=(m // tm, d // tn),
        in_specs=[pl.BlockSpec((tm, d), lambda i, j: (i, 0)),
                  pl.BlockSpec((tm, d), lambda i, j: (i, 0)),
                  pl.BlockSpec((tm, d), lambda i, j: (i, z_col)),
                  pl.BlockSpec((1, HEAD_LANES), lambda i, j: (0, 0)),
                  pl.BlockSpec((d, tn), lambda i, j: (0, j)),
                  pl.BlockSpec((tm, tn), lambda i, j: (i, j)),
                  pl.BlockSpec((1, 1, tn), lambda i, j: (i // tiles_per_group, 0, j))],
        out_specs=pl.BlockSpec((tm, tn), lambda i, j: (i, j)),
        out_shape=jax.ShapeDtypeStruct((m, d), F32),
        scratch_shapes=[pltpu.VMEM((tm, d), BF16)],
        compiler_params=_params("arbitrary", "arbitrary"),
        name=name,
    )(o_f, o_b, proj, onorm_g.reshape(1, HEAD_LANES), w_bf, x2d, gate)


def _chunk_running_sum(x, tri_bf):
    hi = x.astype(BF16)
    lo = (x - hi.astype(F32)).astype(BF16)
    return (jnp.dot(tri_bf, hi, preferred_element_type=F32)
            + jnp.dot(tri_bf, lo, preferred_element_type=F32))


def _scan_kernel(*refs, has_s0, emit_o, emit_s):
    refs = list(refs)
    qf_ref, vf_ref, uf_ref, qb_ref, vb_ref, ub_ref, lb_ref, tri_ref = refs[:8]
    refs = refs[8:]
    s0_ref = refs.pop(0) if has_s0 else None
    of_ref, ob_ref = (refs.pop(0), refs.pop(0)) if emit_o else (None, None)
    sout_ref = refs.pop(0) if emit_s else None
    st_ref, = refs
    n = pl.program_id(2)
    tb = qf_ref.shape[1]

    @pl.when(n == 0)
    def _init():
        if has_s0:
            st_ref[...] = s0_ref[:, 0, 0]
        else:
            st_ref[...] = jnp.zeros_like(st_ref)

    sources = ((qf_ref, vf_ref, uf_ref, of_ref), (qb_ref, vb_ref, ub_ref, ob_ref))
    blk = tri_ref.shape[1]
    n_sub = tb // blk
    n_chunks = blk // HGRN_CHUNK
    work = []
    for d in range(2):
        for sb in (range(n_sub - 1, -1, -1) if d == 1 else range(n_sub)):
            work.append(dict(d=d, rows=slice(sb * blk, (sb + 1) * blk), base=sb * blk))

    for w in work:
        d = w["d"]
        lbd = lb_ref[d]
        sig = jax.nn.sigmoid(sources[d][2][0, w["rows"], :].astype(F32))
        logf = jnp.log(lbd + (1.0 - lbd) * sig)
        w["k"] = (1.0 - lbd) * (1.0 - sig)
        w["b"] = _chunk_running_sum(logf, tri_ref[d].astype(BF16))

    for w in work:
        d = w["d"]
        q_ref, v_ref = sources[d][:2]
        decay = jnp.exp(w["b"])
        k_inv = w["k"] / decay
        v_bf = v_ref[0, w["rows"], :]
        q_dec = (q_ref[0, w["rows"], :].astype(F32) * decay).astype(BF16)
        if emit_o:
            w["a"] = lax.dot_general(q_dec, k_inv.astype(BF16), (((1,), (1,)), ((), ())),
                                     preferred_element_type=F32)
        decay_t = decay.T
        chunks = []
        for c in (range(n_chunks - 1, -1, -1) if d == 1 else range(n_chunks)):
            rows = slice(c * HGRN_CHUNK, (c + 1) * HGRN_CHUNK)
            last = c * HGRN_CHUNK if d == 1 else (c + 1) * HGRN_CHUNK - 1
            k_end = (k_inv[rows] * decay[last:last + 1, :]).astype(BF16)
            kv = lax.dot_general(k_end, v_bf[rows], (((0,), (0,)), ((), ())),
                                 preferred_element_type=F32)
            chunks.append((rows, decay_t[:, last:last + 1], kv))
        w.update(q_dec=q_dec, v_bf=v_bf, chunks=chunks)

    if emit_o:
        for w in work:
            a = jnp.where(tri_ref[w["d"]] > 0.5, w["a"], 0.0).astype(BF16)
            w["o_intra"] = jnp.dot(a, w["v_bf"], preferred_element_type=F32)

    readouts = [[], []]
    for d in range(2):
        st = st_ref[d]
        for w in work:
            if w["d"] != d:
                continue
            for rows, decay_end, kv in w["chunks"]:
                readouts[d].append((w, rows, st.astype(BF16)))
                st = decay_end * st + kv
        st_ref[d] = st

    if emit_o:
        for pair in zip(*readouts):
            for d, (w, rows, st_bf) in enumerate(pair):
                o = w["o_intra"][rows] + jnp.dot(w["q_dec"][rows], st_bf, preferred_element_type=F32)
                sources[d][3][0, w["base"] + rows.start:w["base"] + rows.stop, :] = o.astype(BF16)

    if emit_s:
        @pl.when(n == pl.num_programs(2) - 1)
        def _final():
            sout_ref[:, 0, 0] = st_ref[...]


def _scan(proj, lb, s0, *, tb, n_heads, emit_o, emit_s, name):
    bsz, length, _ = proj.shape
    nb = length // tb
    hd = HEAD_LANES
    fwd = lambda off: pl.BlockSpec((1, tb, hd), lambda b, h, n: (b, n, off * n_heads + h))
    bwd = lambda off: pl.BlockSpec((1, tb, hd), lambda b, h, n: (b, nb - 1 - n, off * n_heads + h))
    state = pl.BlockSpec((2, 1, 1, hd, hd), lambda b, h, n: (0, b, h, 0, 0))
    blk = min(tb, SCAN_BLOCK)
    tok = jnp.arange(blk)
    same_chunk = (tok[:, None] // HGRN_CHUNK) == (tok[None, :] // HGRN_CHUNK)
    tri = jnp.stack([same_chunk & (tok[None, :] <= tok[:, None]),
                     same_chunk & (tok[None, :] >= tok[:, None])]).astype(F32)
    in_specs = [fwd(0), fwd(1), fwd(2), bwd(0), bwd(1), bwd(3),
                pl.BlockSpec((2, 1, hd), lambda b, h, n: (0, 0, h)),
                pl.BlockSpec((2, blk, blk), lambda b, h, n: (0, 0, 0))]
    args = [proj] * 6 + [lb, tri]
    if s0 is not None:
        in_specs.append(state)
        args.append(s0)
    out_specs, out_shape = [], []
    if emit_o:
        out_specs += [pl.BlockSpec((1, tb, hd), lambda b, h, n: (b, n, h)),
                      pl.BlockSpec((1, tb, hd), lambda b, h, n: (b, nb - 1 - n, h))]
        out_shape += [jax.ShapeDtypeStruct((bsz, length, n_heads * hd), BF16)] * 2
    if emit_s:
        out_specs.append(state)
        out_shape.append(jax.ShapeDtypeStruct((2, bsz, n_heads, hd, hd), F32))
    kern = functools.partial(_scan_kernel, has_s0=s0 is not None, emit_o=emit_o, emit_s=emit_s)
    return pl.pallas_call(
        kern,
        grid=(bsz, n_heads, nb),
        in_specs=in_specs,
        out_specs=out_specs,
        out_shape=out_shape,
        scratch_shapes=[pltpu.VMEM((2, hd, hd), F32)],
        compiler_params=_params("arbitrary", "arbitrary", "arbitrary"),
        name=name,
    )(*args)


def _rope_tables(length):
    half = HEAD_DIM // 2
    rows = length // GRID_W
    row = jnp.repeat(jnp.arange(rows), GRID_W).astype(F32)
    col = jnp.tile(jnp.arange(GRID_W), rows).astype(F32)
    inv = ROPE_THETA ** (-jnp.arange(0, half, 2, dtype=F32) / half)
    def axis_angles(pos):
        a = pos[:, None] * inv[None, :]
        return jnp.concatenate([a, a], axis=-1)
    ang = jnp.concatenate([axis_angles(row), axis_angles(col)], axis=-1)
    ang = jnp.concatenate([ang, ang], axis=-1)
    first_half = (jnp.arange(HEAD_LANES) % half) < half // 2
    sin = jnp.sin(ang)
    return jnp.cos(ang), jnp.where(first_half, -sin, 0.0), jnp.where(first_half, 0.0, sin)


def kernel(x, c, ctx, c_ctx, w_ada, b_ada, norm_g, w_in_ab, w_out_ab, qn_g, kn_g, lam_q1, lam_k1,
           lam_q2, lam_k2, subln_g, conv_w, conv_b, cln_g, cln_b, w_in_c, w_out_c, lb_gamma, onorm_g):
    bsz, seq, d = x.shape
    n_ctx = ctx.shape[1]
    depth = w_ada.shape[0]
    assert depth == 2 and w_in_ab.shape[0] == 1 and w_in_c.shape[0] == 1
    att_width = N_HEADS * HEAD_LANES
    conv_width = conv_w.shape[-1]
    assert w_in_ab.shape[-1] == 4 * att_width + 3 * conv_width and conv_width == att_width

    cond_rows = jnp.concatenate([c, c_ctx[None, :], jnp.zeros((8 - bsz - 1, d), F32)], axis=0)
    mod = _ada(cond_rows, w_ada, b_ada)
    def mods(l, rows):
        m = mod[l, rows][:, None, :]
        return m[..., :d], m[..., d:2 * d], m[..., 2 * d:]

    cos, sin_a, sin_b = _rope_tables(seq)
    gmat = jnp.kron(jnp.eye(2, dtype=F32), jnp.full((HEAD_DIM, HEAD_DIM), 1.0 / HEAD_DIM, F32)
                    ).astype(BF16)
    two = lambda g: jnp.concatenate([g, g]).reshape(1, HEAD_LANES)
    q_scale = HEAD_DIM ** -0.5

    x2d = x.reshape(bsz * seq, d)
    ctx2d = ctx.reshape(bsz * n_ctx, d)
    lat_rows = slice(0, bsz)
    ctx_rows = slice(bsz, bsz + 1)

    lam_init = 0.8 - 0.6 * math.exp(-0.3 * 0)
    lam = (jnp.exp(jnp.sum(lam_q1[0] * lam_k1[0])) - jnp.exp(jnp.sum(lam_q2[0] * lam_k2[0]))
           + lam_init).reshape(1)
    shift, scale, gate = mods(0, lat_rows)
    shift_c, scale_c, gate_c = mods(0, ctx_rows)
    w_in = w_in_ab[0].astype(BF16)
    w_out = w_out_ab[0].astype(BF16)
    ab_in = w_in.shape[1]
    qk_lat = (two(qn_g[0]), two(kn_g[0]), gmat, cos, sin_a, sin_b, att_width, q_scale)
    qk_ctx = (two(qn_g[0]), two(kn_g[0]), gmat, None, None, None, att_width, q_scale)
    gates = ((3 * att_width, 4 * att_width), (ab_in - conv_width, ab_in))
    p_lat, vt_lat = _proj(x2d, norm_g[0], scale, shift, w_in, seq, tm=1024, tn=1024, qk=qk_lat,
                          silu_cols=gates, seq_len=seq, name="proj_ab_lat")
    p_ctx, vt_ctx = _proj(ctx2d, norm_g[0], scale_c, shift_c, w_in, bsz * n_ctx, tm=1024, tn=1024,
                          qk=qk_ctx, silu_cols=gates, seq_len=n_ctx, name="proj_ab_ctx")
    p_lat = p_lat.reshape(bsz, seq, ab_in)
    p_ctx = p_ctx.reshape(bsz, n_ctx, ab_in)
    bound = math.sqrt(HEAD_DIM) * jnp.max(jnp.abs(qn_g[0])) * jnp.max(jnp.abs(kn_g[0]))
    scal = jnp.concatenate([lam, bound.reshape(1)])
    def attend(bounded):
        def run():
            lat = _attention(scal, p_lat, p_ctx, vt_ctx, p_lat, vt_lat, subln_g[0], tq=2048,
                             att_width=att_width, lam_init=lam_init, bounded=bounded,
                             name="attn_lat_bounded" if bounded else "attn_lat")
            ctx_ = _attention(scal, p_ctx, p_ctx, vt_ctx, None, None, subln_g[0], tq=n_ctx,
                              att_width=att_width, lam_init=lam_init, bounded=bounded,
                              name="attn_ctx_bounded" if bounded else "attn_ctx")
            return lat, ctx_
        return run
    att_lat, att_ctx = lax.cond(bound <= MAX_FIXED_SHIFT, attend(True), attend(False))
    conv_col0 = 4 * att_width // conv_width
    conv_lat = _conv_branch(p_lat, conv_w[0], conv_b[0], cln_g[0], cln_b[0], tl=256,
                            col0=conv_col0, width=conv_width, name="conv_lat")
    conv_ctx = _conv_branch(p_ctx, conv_w[0], conv_b[0], cln_g[0], cln_b[0], tl=256,
                            col0=conv_col0, width=conv_width, name="conv_ctx")
    x2d = _out_ab(att_lat.reshape(bsz * seq, att_width), conv_lat.reshape(bsz * seq, conv_width),
                  w_out, x2d, gate, seq, tm=512, tn=2048, name="out_ab_lat")
    ctx2d = _out_ab(att_ctx.reshape(bsz * n_ctx, att_width),
                    conv_ctx.reshape(bsz * n_ctx, conv_width),
                    w_out, ctx2d, gate_c, bsz * n_ctx, tm=512, tn=1024, name="out_ab_ctx")

    p = jax.nn.softmax(lb_gamma.astype(F32), axis=1)
    lb_all = jnp.cumsum(p, axis=1) - p[:, :1]
    lb = lb_all[:, 1].reshape(2, 1, d)
    shift, scale, gate = mods(1, lat_rows)
    shift_c, scale_c, _ = mods(1, ctx_rows)
    w_in = w_in_c[0].astype(BF16)
    w_out = w_out_c[0].astype(BF16)
    c_in = w_in.shape[1]
    gates = ((0, d), (4 * d, 5 * d))
    p_lat = _proj(x2d, norm_g[1], scale, shift, w_in, seq, tm=1024, tn=1024, silu_cols=gates,
                  name="proj_c_lat").reshape(bsz, seq, c_in)
    p_ctx = _proj(ctx2d, norm_g[1], scale_c, shift_c, w_in, bsz * n_ctx, tm=1024, tn=1024,
                  silu_cols=gates, name="proj_c_ctx").reshape(bsz, n_ctx, c_in)
    s_ctx, = _scan(p_ctx, lb, None, tb=n_ctx, n_heads=HGRN_HEADS, emit_o=False, emit_s=True,
                   name="scan_ctx")
    o_f, o_b = _scan(p_lat, lb, s_ctx, tb=2048, n_heads=HGRN_HEADS, emit_o=True, emit_s=False,
                     name="scan_lat")
    out = _out_c(o_f.reshape(bsz * seq, d), o_b.reshape(bsz * seq, d),
                 p_lat.reshape(bsz * seq, c_in), 4, onorm_g[0], w_out, x2d, gate, seq,
                 tm=512, tn=2048, name="out_c_lat")
    return out.reshape(bsz, seq, d)
```

```python
import functools
import math

import jax
import jax.numpy as jnp
from jax import lax
from jax.experimental import pallas as pl
from jax.experimental.pallas import tpu as pltpu

F32 = jnp.float32
BF16 = jnp.bfloat16

EPS = 1e-6
GRID_W = 64
ROPE_THETA = 10000.0
HEAD_DIM = 64
HEAD_LANES = 128
N_HEADS = 8
CONV_K = 31
CONV_HALO = 16
SUBLANES = 8
HGRN_HEADS = 16
HGRN_CHUNK = 64
SCAN_BLOCK = 256

KEY_CHUNK = 256
Q_SUB = 256
MAX_FIXED_SHIFT = 40.0
VMEM_LIMIT = 48 * 1024 * 1024


def _silu(x):
    return x * jax.nn.sigmoid(x)


def _params(*sem):
    return pltpu.CompilerParams(dimension_semantics=sem, vmem_limit_bytes=VMEM_LIMIT)


def _ada_kernel(c_ref, w_ref, b_ref, o_ref):
    sc = _silu(c_ref[...])
    o_ref[0] = jnp.dot(sc, w_ref[0], precision=lax.Precision.HIGHEST,
                       preferred_element_type=F32) + b_ref[0]


def _ada(cond_rows, w_ada, b_ada):
    depth, d, n = w_ada.shape
    tn = 768
    return pl.pallas_call(
        _ada_kernel,
        grid=(depth, n // tn),
        in_specs=[pl.BlockSpec((8, d), lambda l, j: (0, 0)),
                  pl.BlockSpec((1, d, tn), lambda l, j: (l, 0, j)),
                  pl.BlockSpec((1, 1, tn), lambda l, j: (l, 0, j))],
        out_specs=pl.BlockSpec((1, 8, tn), lambda l, j: (l, 0, j)),
        out_shape=jax.ShapeDtypeStruct((depth, 8, n), F32),
        compiler_params=_params("arbitrary", "arbitrary"),
        name="ada",
    )(cond_rows, w_ada, b_ada.reshape(depth, 1, n))


def _group_mean_sq(x, gmat):
    sq = x * x
    hi = sq.astype(BF16)
    lo = (sq - hi.astype(F32)).astype(BF16)
    return (jnp.dot(hi, gmat, preferred_element_type=F32)
            + jnp.dot(lo, gmat, preferred_element_type=F32))


def _proj_kernel(*refs, n_qk_tiles, silu_tiles, rope, q_scale, row_chunk, ep_rows):
    if n_qk_tiles:
        (x_ref, g_ref, sc_ref, sh_ref, w_ref, qg_ref, kg_ref, gmat_ref,
         cos_ref, sa_ref, sb_ref, o_ref, vt_ref, xn_ref, acc_ref) = refs
    else:
        x_ref, g_ref, sc_ref, sh_ref, w_ref, o_ref, xn_ref = refs
    j = pl.program_id(1)
    tm, tn = o_ref.shape

    @pl.when(j == 0)
    def _normalise():
        gain = g_ref[...] * (1.0 + sc_ref[0])
        shift = sh_ref[0]

        def body(r, carry):
            r0 = pl.multiple_of(r * row_chunk, row_chunk)
            x = x_ref[pl.ds(r0, row_chunk), :]
            ms = jnp.mean(x * x, axis=-1, keepdims=True)
            xn_ref[pl.ds(r0, row_chunk), :] = (x * lax.rsqrt(ms + EPS) * gain + shift).astype(BF16)
            return carry

        lax.fori_loop(0, tm // row_chunk, body, 0, unroll=2)

    def matmul():
        return jnp.dot(xn_ref[...], w_ref[...], preferred_element_type=F32)

    def norm_rope(gain_ref, scale):
        acc_ref[...] = matmul()
        gmat = gmat_ref[...]
        gain = gain_ref[...]

        def body(r, carry):
            rows = pl.ds(pl.multiple_of(r * ep_rows, ep_rows), ep_rows)
            for c in range(tn // HEAD_LANES):
                cols = slice(c * HEAD_LANES, (c + 1) * HEAD_LANES)
                xs = acc_ref[rows, cols]
                y = xs * lax.rsqrt(_group_mean_sq(xs, gmat) + EPS) * gain
                if rope:
                    y = (y * cos_ref[rows, :]
                         + pltpu.roll(y, HEAD_LANES - 16, axis=1) * sa_ref[rows, :]
                         + pltpu.roll(y, 16, axis=1) * sb_ref[rows, :])
                if scale != 1.0:
                    y = y * scale
                o_ref[rows, cols] = y.astype(BF16)
            return carry

        lax.fori_loop(0, tm // ep_rows, body, 0)

    plain = j >= 2 * n_qk_tiles
    if n_qk_tiles:
        @pl.when(j < n_qk_tiles)
        def _q():
            norm_rope(qg_ref, q_scale)

        @pl.when((j >= n_qk_tiles) & (j < 2 * n_qk_tiles))
        def _k():
            norm_rope(kg_ref, 1.0)

        plain = j > 2 * n_qk_tiles

        @pl.when(j == 2 * n_qk_tiles)
        def _values():
            acc = matmul()
            o_ref[...] = acc.astype(BF16)
            n_seq, n_heads, chunks_per_seq = vt_ref.shape[:3]
            for s in range(n_seq):
                for c in range(chunks_per_seq):
                    r0 = (s * chunks_per_seq + c) * KEY_CHUNK
                    for h in range(n_heads):
                        blk = acc[r0:r0 + KEY_CHUNK, h * HEAD_LANES:(h + 1) * HEAD_LANES]
                        vt_ref[s, h, c] = blk.T.astype(BF16)

    gated = functools.reduce(lambda a, b: a | b, [j == t for t in silu_tiles], j < 0)

    @pl.when(plain & gated)
    def _gated():
        o_ref[...] = _silu(matmul()).astype(BF16)

    @pl.when(plain & jnp.logical_not(gated))
    def _rest():
        o_ref[...] = matmul().astype(BF16)


def _proj(x2d, gain, scale, shift, w_bf, rows_per_group, *, tm, tn, qk=None, silu_cols=(), seq_len=None,
          name):
    m, d = x2d.shape
    n = w_bf.shape[1]
    tiles_per_group = rows_per_group // tm
    in_specs = [
        pl.BlockSpec((tm, d), lambda i, j: (i, 0)),
        pl.BlockSpec((1, d), lambda i, j: (0, 0)),
        pl.BlockSpec((1, 1, d), lambda i, j: (i // tiles_per_group, 0, 0)),
        pl.BlockSpec((1, 1, d), lambda i, j: (i // tiles_per_group, 0, 0)),
        pl.BlockSpec((d, tn), lambda i, j: (0, j)),
    ]
    args = [x2d, gain.reshape(1, d), scale, shift, w_bf]
    scratch = [pltpu.VMEM((tm, d), BF16)]
    out_specs = [pl.BlockSpec((tm, tn), lambda i, j: (i, j))]
    out_shape = [jax.ShapeDtypeStruct((m, n), BF16)]
    n_qk_tiles, rope, q_scale = 0, False, 1.0
    if qk is not None:
        qg, kg, gmat, cos, sa, sb, q_width, q_scale = qk
        n_qk_tiles = q_width // tn
        rope = cos is not None
        const = lambda i, j: (0, 0)
        in_specs += [pl.BlockSpec((1, HEAD_LANES), const), pl.BlockSpec((1, HEAD_LANES), const),
                     pl.BlockSpec((HEAD_LANES, HEAD_LANES), const)]
        args += [qg, kg, gmat]
        if rope:
            pos_tiles = cos.shape[0] // tm
            pos = lambda i, j: (i % pos_tiles, 0)
            in_specs += [pl.BlockSpec((tm, HEAD_LANES), pos)] * 3
            args += [cos, sa, sb]
        else:
            dummy = jnp.zeros((8, HEAD_LANES), F32)
            in_specs += [pl.BlockSpec((8, HEAD_LANES), const)] * 3
            args += [dummy, dummy, dummy]
        scratch.append(pltpu.VMEM((tm, tn), F32))
        assert tn == q_width
        heads = tn // HEAD_LANES
        seq_rows = min(tm, seq_len)
        tiles_per_seq = seq_len // seq_rows
        vt_block = (tm // seq_rows, heads, seq_rows // KEY_CHUNK, HEAD_LANES, KEY_CHUNK)
        out_specs.append(pl.BlockSpec(
            vt_block, lambda i, j: (i // tiles_per_seq, 0, i % tiles_per_seq, 0, 0)))
        out_shape.append(jax.ShapeDtypeStruct(
            (m // seq_len, heads, seq_len // KEY_CHUNK, HEAD_LANES, KEY_CHUNK), BF16))
    silu_tiles = tuple(t for a, b in silu_cols for t in range(a // tn, b // tn))
    kern = functools.partial(_proj_kernel, n_qk_tiles=n_qk_tiles, silu_tiles=silu_tiles,
                             rope=rope, q_scale=q_scale, row_chunk=32, ep_rows=256)
    outs = pl.pallas_call(
        kern,
        grid=(m // tm, n // tn),
        in_specs=in_specs,
        out_specs=out_specs,
        out_shape=out_shape,
        scratch_shapes=scratch,
        compiler_params=_params("arbitrary", "arbitrary"),
        name=name,
    )(*args)
    return outs if qk is not None else outs[0]


def _attn_refs(refs, has_lat):
    if has_lat:
        return refs
    scal_ref, q_ref, kc_ref, vtc_ref, za_ref, sg_ref, o_ref = refs
    return scal_ref, q_ref, kc_ref, vtc_ref, None, None, za_ref, sg_ref, o_ref


def _key_value_chunks(kc_ref, vtc_ref, kl_ref, vtl_ref):
    n_ctx_chunks = vtc_ref.shape[2]
    n_lat_chunks = 0 if vtl_ref is None else vtl_ref.shape[2]

    def ctx(c):
        return kc_ref[0, c * KEY_CHUNK:(c + 1) * KEY_CHUNK, :], vtc_ref[0, 0, c]

    def lat(c):
        return kl_ref[0, pl.ds(pl.multiple_of(c * KEY_CHUNK, KEY_CHUNK), KEY_CHUNK), :], vtl_ref[0, 0, c]

    return n_ctx_chunks, n_lat_chunks, ctx, lat


def _query_maps(q):
    lane = lax.broadcasted_iota(jnp.int32, q.shape, 1)
    zero = jnp.zeros_like(q)
    return jnp.where(lane < HEAD_DIM, q, zero), jnp.where(lane >= HEAD_DIM, q, zero)


def _attn_finish(o_t, za, sg_ref, lam_init):
    o = o_t.T
    ms = jnp.mean(o * o, axis=-1, keepdims=True)
    y = o * lax.rsqrt(ms + EPS) * sg_ref[...] * (1.0 - lam_init)
    return (y * za.astype(F32)).astype(BF16)


def _attn_kernel(*refs, has_lat, lam_init):
    (scal_ref, q_ref, kc_ref, vtc_ref, kl_ref, vtl_ref, za_ref, sg_ref,
     o_ref) = _attn_refs(refs, has_lat)
    tq = q_ref.shape[1]
    n_ctx_chunks, n_lat_chunks, ctx_chunk, lat_chunk = _key_value_chunks(kc_ref, vtc_ref, kl_ref, vtl_ref)
    q_maps = _query_maps(q_ref[0])

    def step(kblk, vtb, carry):
        out = []
        for mp in range(2):
            m_old, l_old, o_old = carry[3 * mp:3 * mp + 3]
            s = lax.dot_general(kblk, q_maps[mp], (((1,), (1,)), ((), ())),
                                preferred_element_type=F32)
            m_new = jnp.maximum(m_old, jnp.max(s, axis=0, keepdims=True))
            p = jnp.exp(s - m_new)
            alpha = jnp.exp(m_old - m_new)
            l_new = alpha * l_old + jnp.sum(p, axis=0, keepdims=True)
            o_new = alpha * o_old + jnp.dot(vtb, p.astype(BF16), preferred_element_type=F32)
            out += [m_new, l_new, o_new]
        return tuple(out)

    m0 = jnp.full((1, tq), -1e30, F32)
    l0 = jnp.zeros((1, tq), F32)
    o0 = jnp.zeros((HEAD_LANES, tq), F32)
    carry = (m0, l0, o0, m0, l0, o0)
    for c in range(n_ctx_chunks):
        carry = step(*ctx_chunk(c), carry)
    if n_lat_chunks:
        carry = lax.fori_loop(0, n_lat_chunks, lambda c, acc: step(*lat_chunk(c), acc), carry)
    _, l_a, o_a, _, l_b, o_b = carry
    o_ref[0] = _attn_finish(o_a / l_a - scal_ref[0] * (o_b / l_b), za_ref[0], sg_ref, lam_init)


def _attn_bounded_kernel(*refs, has_lat, lam_init):
    (scal_ref, q_ref, kc_ref, vtc_ref, kl_ref, vtl_ref, za_ref, sg_ref,
     o_ref) = _attn_refs(refs, has_lat)
    tq = q_ref.shape[1]
    n_ctx_chunks, n_lat_chunks, ctx_chunk, lat_chunk = _key_value_chunks(kc_ref, vtc_ref, kl_ref, vtl_ref)
    n_chunks = n_ctx_chunks + n_lat_chunks

    def chunk(c):
        return ctx_chunk(c) if c < n_ctx_chunks else lat_chunk(c - n_ctx_chunks)

    shift = scal_ref[1]
    steps = [(t, c) for t in range(tq // Q_SUB) for c in range(n_chunks)]
    q_maps = {}

    def scores(t, c):
        if t not in q_maps:
            q_maps[t] = [qm.astype(F32).T.astype(BF16)
                         for qm in _query_maps(q_ref[0, t * Q_SUB:(t + 1) * Q_SUB, :])]
        kblk = chunk(c)[0]
        return [jnp.dot(kblk, q_maps[t][mp], preferred_element_type=F32)
                for mp in range(2)]

    s_next = scores(*steps[0])
    for i, (t, c) in enumerate(steps):
        if c == 0:
            l_acc = [jnp.zeros((SUBLANES, Q_SUB), F32) for _ in range(2)]
            o_acc = [jnp.zeros((HEAD_LANES, Q_SUB), F32) for _ in range(2)]
        s_cur = s_next
        if i + 1 < len(steps):
            s_next = scores(*steps[i + 1])
        vtb = chunk(c)[1]
        for mp in range(2):
            p = jnp.exp(s_cur[mp] - shift)
            l_acc[mp] = l_acc[mp] + jnp.sum(p.reshape(KEY_CHUNK // SUBLANES, SUBLANES, Q_SUB), axis=0)
            o_acc[mp] = o_acc[mp] + jnp.dot(vtb, p.astype(BF16), preferred_element_type=F32)
        if c == n_chunks - 1:
            l_a = jnp.sum(l_acc[0], axis=0, keepdims=True)
            l_b = jnp.sum(l_acc[1], axis=0, keepdims=True)
            rows = slice(t * Q_SUB, (t + 1) * Q_SUB)
            o_ref[0, rows, :] = _attn_finish(o_acc[0] / l_a - scal_ref[0] * (o_acc[1] / l_b),
                                             za_ref[0, rows, :], sg_ref, lam_init)


def _attention(scal, proj_q, proj_ctx, vt_ctx, proj_lat, vt_lat, subln_g, *, tq, att_width, lam_init,
               bounded, name):
    bsz, lq, _ = proj_q.shape
    n_ctx = proj_ctx.shape[1]
    nh = att_width // HEAD_LANES
    k_off, z_off = nh, 3 * nh
    vt_spec = lambda vt: pl.BlockSpec((1, 1) + vt.shape[2:], lambda b, h, i: (b, h, 0, 0, 0))
    in_specs = [
        pl.BlockSpec(memory_space=pltpu.SMEM),
        pl.BlockSpec((1, tq, HEAD_LANES), lambda b, h, i: (b, i, h)),
        pl.BlockSpec((1, n_ctx, HEAD_LANES), lambda b, h, i: (b, 0, k_off + h)),
        vt_spec(vt_ctx),
    ]
    args = [scal, proj_q, proj_ctx, vt_ctx]
    if proj_lat is not None:
        n_lat = proj_lat.shape[1]
        in_specs += [pl.BlockSpec((1, n_lat, HEAD_LANES), lambda b, h, i: (b, 0, k_off + h)),
                     vt_spec(vt_lat)]
        args += [proj_lat, vt_lat]
    in_specs += [pl.BlockSpec((1, tq, HEAD_LANES), lambda b, h, i: (b, i, z_off + h)),
                 pl.BlockSpec((1, HEAD_LANES), lambda b, h, i: (0, 0))]
    args += [proj_q, subln_g.reshape(1, HEAD_LANES)]
    kern = functools.partial(_attn_bounded_kernel if bounded else _attn_kernel,
                             has_lat=proj_lat is not None, lam_init=lam_init)
    return pl.pallas_call(
        kern,
        grid=(bsz, nh, lq // tq),
        in_specs=in_specs,
        out_specs=pl.BlockSpec((1, tq, HEAD_LANES), lambda b, h, i: (b, i, h)),
        out_shape=jax.ShapeDtypeStruct((bsz, lq, att_width), BF16),
        compiler_params=_params("arbitrary", "arbitrary", "arbitrary"),
        name=name,
    )(*args)


def _conv_kernel(gv_ref, gg_ref, zb_ref, gvp_ref, ggp_ref, gvn_ref, ggn_ref,
                 w_ref, b_ref, lg_ref, lb_ref, o_ref, ypad_ref, ysh_ref, *, row_chunk):
    i = pl.program_id(1)
    n_tiles = pl.num_programs(1)
    tl = o_ref.shape[1]
    padded = tl + 2 * CONV_HALO

    def glu(val_ref, gate_ref, rows):
        return val_ref[0, rows, :].astype(F32) * jax.nn.sigmoid(gate_ref[0, rows, :].astype(F32))

    halo = slice(0, CONV_HALO)
    ypad_ref[0:CONV_HALO, :] = jnp.where(i > 0, glu(gvp_ref, ggp_ref, halo), 0.0)
    ypad_ref[CONV_HALO + tl:padded, :] = jnp.where(i < n_tiles - 1, glu(gvn_ref, ggn_ref, halo), 0.0)
    for r in range(tl // row_chunk):
        rows = slice(r * row_chunk, (r + 1) * row_chunk)
        ypad_ref[CONV_HALO + r * row_chunk:CONV_HALO + (r + 1) * row_chunk, :] = glu(gv_ref, gg_ref, rows)

    span = padded - SUBLANES
    for s in range(1, SUBLANES):
        for r in range(0, span, row_chunk):
            n = min(row_chunk, span - r)
            ysh_ref[s, r:r + n, :] = ypad_ref[r + s:r + s + n, :]

    first = CONV_HALO - CONV_K // 2
    for r in range(tl // row_chunk):
        acc = None
        for t in range(CONV_K):
            start = r * row_chunk + first + t
            phase, base = start % SUBLANES, start - start % SUBLANES
            src = ypad_ref[base:base + row_chunk, :] if phase == 0 else ysh_ref[phase, base:base + row_chunk, :]
            term = src.reshape(row_chunk // SUBLANES, SUBLANES, -1) * w_ref[t][None]
            acc = term if acc is None else acc + term
        acc = acc.reshape(row_chunk, -1) + b_ref[...]
        mu = jnp.mean(acc, axis=-1, keepdims=True)
        xc = acc - mu
        var = jnp.mean(xc * xc, axis=-1, keepdims=True)
        y = xc * lax.rsqrt(var + EPS) * lg_ref[...] + lb_ref[...]
        rows = slice(r * row_chunk, (r + 1) * row_chunk)
        o_ref[0, rows, :] = (_silu(y) * zb_ref[0, rows, :].astype(F32)).astype(BF16)


def _conv_branch(proj, conv_w, conv_b, cln_g, cln_b, *, tl, col0, width, name):
    bsz, length, _ = proj.shape
    n_tiles = length // tl
    halo_per_tile = tl // CONV_HALO
    last_halo = length // CONV_HALO - 1
    main = lambda off: pl.BlockSpec((1, tl, width), lambda b, i: (b, i, col0 + off))
    prev = lambda off: pl.BlockSpec(
        (1, CONV_HALO, width), lambda b, i: (b, jnp.maximum(i * halo_per_tile - 1, 0), col0 + off))
    nxt = lambda off: pl.BlockSpec(
        (1, CONV_HALO, width),
        lambda b, i: (b, jnp.minimum((i + 1) * halo_per_tile, last_halo), col0 + off))
    row = lambda k: pl.BlockSpec((k, width), lambda b, i: (0, 0))
    padded = tl + 2 * CONV_HALO
    return pl.pallas_call(
        functools.partial(_conv_kernel, row_chunk=32),
        grid=(bsz, n_tiles),
        in_specs=[main(0), main(1), main(2), prev(0), prev(1), nxt(0), nxt(1),
                  pl.BlockSpec((CONV_K, SUBLANES, width), lambda b, i: (0, 0, 0)),
                  row(1), row(1), row(1)],
        out_specs=pl.BlockSpec((1, tl, width), lambda b, i: (b, i, 0)),
        out_shape=jax.ShapeDtypeStruct((bsz, length, width), BF16),
        scratch_shapes=[pltpu.VMEM((padded, width), F32),
                        pltpu.VMEM((SUBLANES, padded, width), F32)],
        compiler_params=_params("arbitrary", "arbitrary"),
        name=name,
    )(proj, proj, proj, proj, proj, proj, proj,
      jnp.broadcast_to(conv_w[:, None, :], (CONV_K, SUBLANES, width)),
      conv_b.reshape(1, width), cln_g.reshape(1, width), cln_b.reshape(1, width))


def _out_ab_kernel(a_ref, c_ref, wa_ref, wc_ref, x_ref, gate_ref, o_ref):
    y = (jnp.dot(a_ref[...], wa_ref[...], preferred_element_type=F32)
         + jnp.dot(c_ref[...], wc_ref[...], preferred_element_type=F32))
    o_ref[...] = x_ref[...] + gate_ref[0] * y


def _out_ab(att, conv, w_bf, x2d, gate, rows_per_group, *, tm, tn, name):
    m, d = x2d.shape
    ka = att.shape[1]
    kc = conv.shape[1]
    tiles_per_group = rows_per_group // tm
    return pl.pallas_call(
        _out_ab_kernel,
        grid=(m // tm, d // tn),
        in_specs=[pl.BlockSpec((tm, ka), lambda i, j: (i, 0)),
                  pl.BlockSpec((tm, kc), lambda i, j: (i, 0)),
                  pl.BlockSpec((ka, tn), lambda i, j: (0, j)),
                  pl.BlockSpec((kc, tn), lambda i, j: (ka // kc, j)),
                  pl.BlockSpec((tm, tn), lambda i, j: (i, j)),
                  pl.BlockSpec((1, 1, tn), lambda i, j: (i // tiles_per_group, 0, j))],
        out_specs=pl.BlockSpec((tm, tn), lambda i, j: (i, j)),
        out_shape=jax.ShapeDtypeStruct((m, d), F32),
        compiler_params=_params("arbitrary", "arbitrary"),
        name=name,
    )(att, conv, w_bf, w_bf, x2d, gate)


def _out_c_kernel(of_ref, ob_ref, z_ref, g_ref, w_ref, x_ref, gate_ref, o_ref, y_ref, *, row_chunk):
    j = pl.program_id(1)
    tm, d = y_ref.shape

    @pl.when(j == 0)
    def _readout():
        def body(r, carry):
            rows = pl.ds(pl.multiple_of(r * row_chunk, row_chunk), row_chunk)
            for h in range(d // HEAD_LANES):
                cols = slice(h * HEAD_LANES, (h + 1) * HEAD_LANES)
                o = of_ref[rows, cols].astype(F32) + ob_ref[rows, cols].astype(F32)
                ms = jnp.mean(o * o, axis=-1, keepdims=True)
                y = o * lax.rsqrt(ms + EPS) * g_ref[...]
                y_ref[rows, cols] = (y * z_ref[rows, cols].astype(F32)).astype(BF16)
            return carry
        lax.fori_loop(0, tm // row_chunk, body, 0)

    y = jnp.dot(y_ref[...], w_ref[...], preferred_element_type=F32)
    o_ref[...] = x_ref[...] + gate_ref[0] * y


def _out_c(o_f, o_b, proj, z_col, onorm_g, w_bf, x2d, gate, rows_per_group, *, tm, tn, name):
    m, d = x2d.shape
    tiles_per_group = rows_per_group // tm
    return pl.pallas_call(
        functools.partial(_out_c_kernel, row_chunk=64),
        grid=(m // tm, d // tn),
        in_specs=[pl.BlockSpec((tm, d), lambda i, j: (i, 0)),
                  pl.BlockSpec((tm, d), lambda i, j: (i, 0)),
                  pl.BlockSpec((tm, d), lambda i, j: (i, z_col)),
                  pl.BlockSpec((1, HEAD_LANES), lambda i, j: (0, 0)),
                  pl.BlockSpec((d, tn), lambda i, j: (0, j)),
                  pl.BlockSpec((tm, tn), lambda i, j: (i, j)),
                  pl.BlockSpec((1, 1, tn), lambda i, j: (i // tiles_per_group, 0, j))],
        out_specs=pl.BlockSpec((tm, tn), lambda i, j: (i, j)),
        out_shape=jax.ShapeDtypeStruct((m, d), F32),
        scratch_shapes=[pltpu.VMEM((tm, d), BF16)],
        compiler_params=_params("arbitrary", "arbitrary"),
        name=name,
    )(o_f, o_b, proj, onorm_g.reshape(1, HEAD_LANES), w_bf, x2d, gate)


def _chunk_running_sum(x, tri_bf):
    hi = x.astype(BF16)
    lo = (x - hi.astype(F32)).astype(BF16)
    return (jnp.dot(tri_bf, hi, preferred_element_type=F32)
            + jnp.dot(tri_bf, lo, preferred_element_type=F32))


def _scan_kernel(*refs, has_s0, emit_o, emit_s):
    refs = list(refs)
    qf_ref, vf_ref, uf_ref, qb_ref, vb_ref, ub_ref, lb_ref, tri_ref = refs[:8]
    refs = refs[8:]
    s0_ref = refs.pop(0) if has_s0 else None
    of_ref, ob_ref = (refs.pop(0), refs.pop(0)) if emit_o else (None, None)
    sout_ref = refs.pop(0) if emit_s else None
    st_ref, = refs
    n = pl.program_id(2)
    tb = qf_ref.shape[1]

    @pl.when(n == 0)
    def _init():
        if has_s0:
            st_ref[...] = s0_ref[:, 0, 0]
        else:
            st_ref[...] = jnp.zeros_like(st_ref)

    sources = ((qf_ref, vf_ref, uf_ref, of_ref), (qb_ref, vb_ref, ub_ref, ob_ref))
    blk = tri_ref.shape[1]
    n_sub = tb // blk
    n_chunks = blk // HGRN_CHUNK
    work = []
    for d in range(2):
        for sb in (range(n_sub - 1, -1, -1) if d == 1 else range(n_sub)):
            work.append(dict(d=d, rows=slice(sb * blk, (sb + 1) * blk), base=sb * blk))

    for w in work:
        d = w["d"]
        lbd = lb_ref[d]
        sig = jax.nn.sigmoid(sources[d][2][0, w["rows"], :].astype(F32))
        logf = jnp.log(lbd + (1.0 - lbd) * sig)
        w["k"] = (1.0 - lbd) * (1.0 - sig)
        w["b"] = _chunk_running_sum(logf, tri_ref[d].astype(BF16))

    for w in work:
        d = w["d"]
        q_ref, v_ref = sources[d][:2]
        decay = jnp.exp(w["b"])
        k_inv = w["k"] / decay
        v_bf = v_ref[0, w["rows"], :]
        q_dec = (q_ref[0, w["rows"], :].astype(F32) * decay).astype(BF16)
        if emit_o:
            w["a"] = lax.dot_general(q_dec, k_inv.astype(BF16), (((1,), (1,)), ((), ())),
                                     preferred_element_type=F32)
        decay_t = decay.T
        chunks = []
        for c in (range(n_chunks - 1, -1, -1) if d == 1 else range(n_chunks)):
            rows = slice(c * HGRN_CHUNK, (c + 1) * HGRN_CHUNK)
            last = c * HGRN_CHUNK if d == 1 else (c + 1) * HGRN_CHUNK - 1
            k_end = (k_inv[rows] * decay[last:last + 1, :]).astype(BF16)
            kv = lax.dot_general(k_end, v_bf[rows], (((0,), (0,)), ((), ())),
                                 preferred_element_type=F32)
            chunks.append((rows, decay_t[:, last:last + 1], kv))
        w.update(q_dec=q_dec, v_bf=v_bf, chunks=chunks)

    if emit_o:
        for w in work:
            a = jnp.where(tri_ref[w["d"]] > 0.5, w["a"], 0.0).astype(BF16)
            w["o_intra"] = jnp.dot(a, w["v_bf"], preferred_element_type=F32)

    readouts = [[], []]
    for d in range(2):
        st = st_ref[d]
        for w in work:
            if w["d"] != d:
                continue
            for rows, decay_end, kv in w["chunks"]:
                readouts[d].append((w, rows, st.astype(BF16)))
                st = decay_end * st + kv
        st_ref[d] = st

    if emit_o:
        for pair in zip(*readouts):
            for d, (w, rows, st_bf) in enumerate(pair):
                o = w["o_intra"][rows] + jnp.dot(w["q_dec"][rows], st_bf, preferred_element_type=F32)
                sources[d][3][0, w["base"] + rows.start:w["base"] + rows.stop, :] = o.astype(BF16)

    if emit_s:
        @pl.when(n == pl.num_programs(2) - 1)
        def _final():
            sout_ref[:, 0, 0] = st_ref[...]


def _scan(proj, lb, s0, *, tb, n_heads, emit_o, emit_s, name):
    bsz, length, _ = proj.shape
    nb = length // tb
    hd = HEAD_LANES
    fwd = lambda off: pl.BlockSpec((1, tb, hd), lambda b, h, n: (b, n, off * n_heads + h))
    bwd = lambda off: pl.BlockSpec((1, tb, hd), lambda b, h, n: (b, nb - 1 - n, off * n_heads + h))
    state = pl.BlockSpec((2, 1, 1, hd, hd), lambda b, h, n: (0, b, h, 0, 0))
    blk = min(tb, SCAN_BLOCK)
    tok = jnp.arange(blk)
    same_chunk = (tok[:, None] // HGRN_CHUNK) == (tok[None, :] // HGRN_CHUNK)
    tri = jnp.stack([same_chunk & (tok[None, :] <= tok[:, None]),
                     same_chunk & (tok[None, :] >= tok[:, None])]).astype(F32)
    in_specs = [fwd(0), fwd(1), fwd(2), bwd(0), bwd(1), bwd(3),
                pl.BlockSpec((2, 1, hd), lambda b, h, n: (0, 0, h)),
                pl.BlockSpec((2, blk, blk), lambda b, h, n: (0, 0, 0))]
    args = [proj] * 6 + [lb, tri]
    if s0 is not None:
        in_specs.append(state)
        args.append(s0)
    out_specs, out_shape = [], []
    if emit_o:
        out_specs += [pl.BlockSpec((1, tb, hd), lambda b, h, n: (b, n, h)),
                      pl.BlockSpec((1, tb, hd), lambda b, h, n: (b, nb - 1 - n, h))]
        out_shape += [jax.ShapeDtypeStruct((bsz, length, n_heads * hd), BF16)] * 2
    if emit_s:
        out_specs.append(state)
        out_shape.append(jax.ShapeDtypeStruct((2, bsz, n_heads, hd, hd), F32))
    kern = functools.partial(_scan_kernel, has_s0=s0 is not None, emit_o=emit_o, emit_s=emit_s)
    return pl.pallas_call(
        kern,
        grid=(bsz, n_heads, nb),
        in_specs=in_specs,
        out_specs=out_specs,
        out_shape=out_shape,
        scratch_shapes=[pltpu.VMEM((2, hd, hd), F32)],
        compiler_params=_params("arbitrary", "arbitrary", "arbitrary"),
        name=name,
    )(*args)


def _rope_tables(length):
    half = HEAD_DIM // 2
    rows = length // GRID_W
    row = jnp.repeat(jnp.arange(rows), GRID_W).astype(F32)
    col = jnp.tile(jnp.arange(GRID_W), rows).astype(F32)
    inv = ROPE_THETA ** (-jnp.arange(0, half, 2, dtype=F32) / half)
    def axis_angles(pos):
        a = pos[:, None] * inv[None, :]
        return jnp.concatenate([a, a], axis=-1)
    ang = jnp.concatenate([axis_angles(row), axis_angles(col)], axis=-1)
    ang = jnp.concatenate([ang, ang], axis=-1)
    first_half = (jnp.arange(HEAD_LANES) % half) < half // 2
    sin = jnp.sin(ang)
    return jnp.cos(ang), jnp.where(first_half, -sin, 0.0), jnp.where(first_half, 0.0, sin)


def kernel(x, c, ctx, c_ctx, w_ada, b_ada, norm_g, w_in_ab, w_out_ab, qn_g, kn_g, lam_q1, lam_k1,
           lam_q2, lam_k2, subln_g, conv_w, conv_b, cln_g, cln_b, w_in_c, w_out_c, lb_gamma, onorm_g):
    bsz, seq, d = x.shape
    n_ctx = ctx.shape[1]
    depth = w_ada.shape[0]
    assert depth == 2 and w_in_ab.shape[0] == 1 and w_in_c.shape[0] == 1
    att_width = N_HEADS * HEAD_LANES
    conv_width = conv_w.shape[-1]
    assert w_in_ab.shape[-1] == 4 * att_width + 3 * conv_width and conv_width == att_width

    cond_rows = jnp.concatenate([c, c_ctx[None, :], jnp.zeros((8 - bsz - 1, d), F32)], axis=0)
    mod = _ada(cond_rows, w_ada, b_ada)
    def mods(l, rows):
        m = mod[l, rows][:, None, :]
        return m[..., :d], m[..., d:2 * d], m[..., 2 * d:]

    cos, sin_a, sin_b = _rope_tables(seq)
    gmat = jnp.kron(jnp.eye(2, dtype=F32), jnp.full((HEAD_DIM, HEAD_DIM), 1.0 / HEAD_DIM, F32)
                    ).astype(BF16)
    two = lambda g: jnp.concatenate([g, g]).reshape(1, HEAD_LANES)
    q_scale = HEAD_DIM ** -0.5

    x2d = x.reshape(bsz * seq, d)
    ctx2d = ctx.reshape(bsz * n_ctx, d)
    lat_rows = slice(0, bsz)
    ctx_rows = slice(bsz, bsz + 1)

    lam_init = 0.8 - 0.6 * math.exp(-0.3 * 0)
    lam = (jnp.exp(jnp.sum(lam_q1[0] * lam_k1[0])) - jnp.exp(jnp.sum(lam_q2[0] * lam_k2[0]))
           + lam_init).reshape(1)
    shift, scale, gate = mods(0, lat_rows)
    shift_c, scale_c, gate_c = mods(0, ctx_rows)
    w_in = w_in_ab[0].astype(BF16)
    w_out = w_out_ab[0].astype(BF16)
    ab_in = w_in.shape[1]
    qk_lat = (two(qn_g[0]), two(kn_g[0]), gmat, cos, sin_a, sin_b, att_width, q_scale)
    qk_ctx = (two(qn_g[0]), two(kn_g[0]), gmat, None, None, None, att_width, q_scale)
    gates = ((3 * att_width, 4 * att_width), (ab_in - conv_width, ab_in))
    p_lat, vt_lat = _proj(x2d, norm_g[0], scale, shift, w_in, seq, tm=1024, tn=1024, qk=qk_lat,
                          silu_cols=gates, seq_len=seq, name="proj_ab_lat")
    p_ctx, vt_ctx = _proj(ctx2d, norm_g[0], scale_c, shift_c, w_in, bsz * n_ctx, tm=1024, tn=1024,
                          qk=qk_ctx, silu_cols=gates, seq_len=n_ctx, name="proj_ab_ctx")
    p_lat = p_lat.reshape(bsz, seq, ab_in)
    p_ctx = p_ctx.reshape(bsz, n_ctx, ab_in)
    bound = math.sqrt(HEAD_DIM) * jnp.max(jnp.abs(qn_g[0])) * jnp.max(jnp.abs(kn_g[0]))
    scal = jnp.concatenate([lam, bound.reshape(1)])
    def attend(bounded):
        def run():
            lat = _attention(scal, p_lat, p_ctx, vt_ctx, p_lat, vt_lat, subln_g[0], tq=2048,
                             att_width=att_width, lam_init=lam_init, bounded=bounded,
                             name="attn_lat_bounded" if bounded else "attn_lat")
            ctx_ = _attention(scal, p_ctx, p_ctx, vt_ctx, None, None, subln_g[0], tq=n_ctx,
                              att_width=att_width, lam_init=lam_init, bounded=bounded,
                              name="attn_ctx_bounded" if bounded else "attn_ctx")
            return lat, ctx_
        return run
    att_lat, att_ctx = lax.cond(bound <= MAX_FIXED_SHIFT, attend(True), attend(False))
    conv_col0 = 4 * att_width // conv_width
    conv_lat = _conv_branch(p_lat, conv_w[0], conv_b[0], cln_g[0], cln_b[0], tl=256,
                            col0=conv_col0, width=conv_width, name="conv_lat")
    conv_ctx = _conv_branch(p_ctx, conv_w[0], conv_b[0], cln_g[0], cln_b[0], tl=256,
                            col0=conv_col0, width=conv_width, name="conv_ctx")
    x2d = _out_ab(att_lat.reshape(bsz * seq, att_width), conv_lat.reshape(bsz * seq, conv_width),
                  w_out, x2d, gate, seq, tm=512, tn=2048, name="out_ab_lat")
    ctx2d = _out_ab(att_ctx.reshape(bsz * n_ctx, att_width),
                    conv_ctx.reshape(bsz * n_ctx, conv_width),
                    w_out, ctx2d, gate_c, bsz * n_ctx, tm=512, tn=1024, name="out_ab_ctx")

    p = jax.nn.softmax(lb_gamma.astype(F32), axis=1)
    lb_all = jnp.cumsum(p, axis=1) - p[:, :1]
    lb = lb_all[:, 1].reshape(2, 1, d)
    shift, scale, gate = mods(1, lat_rows)
    shift_c, scale_c, _ = mods(1, ctx_rows)
    w_in = w_in_c[0].astype(BF16)
    w_out = w_out_c[0].astype(BF16)
    c_in = w_in.shape[1]
    gates = ((0, d), (4 * d, 5 * d))
    p_lat = _proj(x2d, norm_g[1], scale, shift, w_in, seq, tm=1024, tn=1024, silu_cols=gates,
                  name="proj_c_lat").reshape(bsz, seq, c_in)
    p_ctx = _proj(ctx2d, norm_g[1], scale_c, shift_c, w_in, bsz * n_ctx, tm=1024, tn=1024,
                  silu_cols=gates, name="proj_c_ctx").reshape(bsz, n_ctx, c_in)
    s_ctx, = _scan(p_ctx, lb, None, tb=n_ctx, n_heads=HGRN_HEADS, emit_o=False, emit_s=True,
                   name="scan_ctx")
    o_f, o_b = _scan(p_lat, lb, s_ctx, tb=2048, n_heads=HGRN_HEADS, emit_o=True, emit_s=False,
                     name="scan_lat")
    out = _out_c(o_f.reshape(bsz * seq, d), o_b.reshape(bsz * seq, d),
                 p_lat.reshape(bsz * seq, c_in), 4, onorm_g[0], w_out, x2d, gate, seq,
                 tm=512, tn=2048, name="out_c_lat")
    return out.reshape(bsz, seq, d)
```

```python
import functools
import math

import jax
import jax.numpy as jnp
from jax import lax
from jax.experimental import pallas as pl
from jax.experimental.pallas import tpu as pltpu

F32 = jnp.float32
BF16 = jnp.bfloat16

EPS = 1e-6
GRID_W = 64
ROPE_THETA = 10000.0
HEAD_DIM = 64
HEAD_LANES = 128
N_HEADS = 8
CONV_K = 31
CONV_HALO = 16
SUBLANES = 8
HGRN_HEADS = 16
HGRN_CHUNK = 64
SCAN_BLOCK = 256

KEY_CHUNK = 256
Q_SUB = 256
MAX_FIXED_SHIFT = 40.0
VMEM_LIMIT = 48 * 1024 * 1024


def _silu(x):
    return x * jax.nn.sigmoid(x)


def _params(*sem):
    return pltpu.CompilerParams(dimension_semantics=sem, vmem_limit_bytes=VMEM_LIMIT)


def _ada_kernel(c_ref, w_ref, b_ref, o_ref):
    sc = _silu(c_ref[...]).astype(BF16)
    o_ref[0] = jnp.dot(sc, w_ref[0].astype(BF16), preferred_element_type=F32) + b_ref[0]


def _ada(cond_rows, w_ada, b_ada):
    depth, d, n = w_ada.shape
    tn = 768
    return pl.pallas_call(
        _ada_kernel,
        grid=(depth, n // tn),
        in_specs=[pl.BlockSpec((8, d), lambda l, j: (0, 0)),
                  pl.BlockSpec((1, d, tn), lambda l, j: (l, 0, j)),
                  pl.BlockSpec((1, 1, tn), lambda l, j: (l, 0, j))],
        out_specs=pl.BlockSpec((1, 8, tn), lambda l, j: (l, 0, j)),
        out_shape=jax.ShapeDtypeStruct((depth, 8, n), F32),
        compiler_params=_params("arbitrary", "arbitrary"),
        name="ada",
    )(cond_rows, w_ada, b_ada.reshape(depth, 1, n))


def _group_mean_sq(x, gmat):
    sq = x * x
    hi = sq.astype(BF16)
    lo = (sq - hi.astype(F32)).astype(BF16)
    return (jnp.dot(hi, gmat, preferred_element_type=F32)
            + jnp.dot(lo, gmat, preferred_element_type=F32))


def _proj_kernel(*refs, n_qk_tiles, silu_tiles, rope, q_scale, row_chunk, ep_rows):
    if n_qk_tiles:
        (x_ref, g_ref, sc_ref, sh_ref, w_ref, qg_ref, kg_ref, gmat_ref,
         cos_ref, sa_ref, sb_ref, o_ref, vt_ref, xn_ref, acc_ref) = refs
    else:
        x_ref, g_ref, sc_ref, sh_ref, w_ref, o_ref, xn_ref = refs
    j = pl.program_id(1)
    tm, tn = o_ref.shape

    @pl.when(j == 0)
    def _normalise():
        gain = g_ref[...] * (1.0 + sc_ref[0])
        shift = sh_ref[0]

        def body(r, carry):
            r0 = pl.multiple_of(r * row_chunk, row_chunk)
            x = x_ref[pl.ds(r0, row_chunk), :]
            ms = jnp.mean(x * x, axis=-1, keepdims=True)
            xn_ref[pl.ds(r0, row_chunk), :] = (x * lax.rsqrt(ms + EPS) * gain + shift).astype(BF16)
            return carry

        lax.fori_loop(0, tm // row_chunk, body, 0, unroll=2)

    def matmul():
        return jnp.dot(xn_ref[...], w_ref[...], preferred_element_type=F32)

    def norm_rope(gain_ref, scale):
        acc_ref[...] = matmul()
        gmat = gmat_ref[...]
        gain = gain_ref[...]

        def body(r, carry):
            rows = pl.ds(pl.multiple_of(r * ep_rows, ep_rows), ep_rows)
            for c in range(tn // HEAD_LANES):
                cols = slice(c * HEAD_LANES, (c + 1) * HEAD_LANES)
                xs = acc_ref[rows, cols]
                y = xs * lax.rsqrt(_group_mean_sq(xs, gmat) + EPS) * gain
                if rope:
                    y = (y * cos_ref[rows, :]
                         + pltpu.roll(y, HEAD_LANES - 16, axis=1) * sa_ref[rows, :]
                         + pltpu.roll(y, 16, axis=1) * sb_ref[rows, :])
                if scale != 1.0:
                    y = y * scale
                o_ref[rows, cols] = y.astype(BF16)
            return carry

        lax.fori_loop(0, tm // ep_rows, body, 0)

    plain = j >= 2 * n_qk_tiles
    if n_qk_tiles:
        @pl.when(j < n_qk_tiles)
        def _q():
            norm_rope(qg_ref, q_scale)

        @pl.when((j >= n_qk_tiles) & (j < 2 * n_qk_tiles))
        def _k():
            norm_rope(kg_ref, 1.0)

        plain = j > 2 * n_qk_tiles

        @pl.when(j == 2 * n_qk_tiles)
        def _values():
            acc = matmul()
            o_ref[...] = acc.astype(BF16)
            n_seq, n_heads, chunks_per_seq = vt_ref.shape[:3]
            for s in range(n_seq):
                for c in range(chunks_per_seq):
                    r0 = (s * chunks_per_seq + c) * KEY_CHUNK
                    for h in range(n_heads):
                        blk = acc[r0:r0 + KEY_CHUNK, h * HEAD_LANES:(h + 1) * HEAD_LANES]
                        vt_ref[s, h, c] = blk.T.astype(BF16)

    gated = functools.reduce(lambda a, b: a | b, [j == t for t in silu_tiles], j < 0)

    @pl.when(plain & gated)
    def _gated():
        o_ref[...] = _silu(matmul()).astype(BF16)

    @pl.when(plain & jnp.logical_not(gated))
    def _rest():
        o_ref[...] = matmul().astype(BF16)


def _proj(x2d, gain, scale, shift, w_bf, rows_per_group, *, tm, tn, qk=None, silu_cols=(), seq_len=None,
          name):
    m, d = x2d.shape
    n = w_bf.shape[1]
    tiles_per_group = rows_per_group // tm
    in_specs = [
        pl.BlockSpec((tm, d), lambda i, j: (i, 0)),
        pl.BlockSpec((1, d), lambda i, j: (0, 0)),
        pl.BlockSpec((1, 1, d), lambda i, j: (i // tiles_per_group, 0, 0)),
        pl.BlockSpec((1, 1, d), lambda i, j: (i // tiles_per_group, 0, 0)),
        pl.BlockSpec((d, tn), lambda i, j: (0, j)),
    ]
    args = [x2d, gain.reshape(1, d), scale, shift, w_bf]
    scratch = [pltpu.VMEM((tm, d), BF16)]
    out_specs = [pl.BlockSpec((tm, tn), lambda i, j: (i, j))]
    out_shape = [jax.ShapeDtypeStruct((m, n), BF16)]
    n_qk_tiles, rope, q_scale = 0, False, 1.0
    if qk is not None:
        qg, kg, gmat, cos, sa, sb, q_width, q_scale = qk
        n_qk_tiles = q_width // tn
        rope = cos is not None
        const = lambda i, j: (0, 0)
        in_specs += [pl.BlockSpec((1, HEAD_LANES), const), pl.BlockSpec((1, HEAD_LANES), const),
                     pl.BlockSpec((HEAD_LANES, HEAD_LANES), const)]
        args += [qg, kg, gmat]
        if rope:
            pos_tiles = cos.shape[0] // tm
            pos = lambda i, j: (i % pos_tiles, 0)
            in_specs += [pl.BlockSpec((tm, HEAD_LANES), pos)] * 3
            args += [cos, sa, sb]
        else:
            dummy = jnp.zeros((8, HEAD_LANES), F32)
            in_specs += [pl.BlockSpec((8, HEAD_LANES), const)] * 3
            args += [dummy, dummy, dummy]
        scratch.append(pltpu.VMEM((tm, tn), F32))
        assert tn == q_width
        heads = tn // HEAD_LANES
        seq_rows = min(tm, seq_len)
        tiles_per_seq = seq_len // seq_rows
        vt_block = (tm // seq_rows, heads, seq_rows // KEY_CHUNK, HEAD_LANES, KEY_CHUNK)
        out_specs.append(pl.BlockSpec(
            vt_block, lambda i, j: (i // tiles_per_seq, 0, i % tiles_per_seq, 0, 0)))
        out_shape.append(jax.ShapeDtypeStruct(
            (m // seq_len, heads, seq_len // KEY_CHUNK, HEAD_LANES, KEY_CHUNK), BF16))
    silu_tiles = tuple(t for a, b in silu_cols for t in range(a // tn, b // tn))
    kern = functools.partial(_proj_kernel, n_qk_tiles=n_qk_tiles, silu_tiles=silu_tiles,
                             rope=rope, q_scale=q_scale, row_chunk=32, ep_rows=256)
    outs = pl.pallas_call(
        kern,
        grid=(m // tm, n // tn),
        in_specs=in_specs,
        out_specs=out_specs,
        out_shape=out_shape,
        scratch_shapes=scratch,
        compiler_params=_params("arbitrary", "arbitrary"),
        name=name,
    )(*args)
    return outs if qk is not None else outs[0]


def _attn_refs(refs, has_lat):
    if has_lat:
        return refs
    scal_ref, q_ref, kc_ref, vtc_ref, za_ref, sg_ref, o_ref = refs
    return scal_ref, q_ref, kc_ref, vtc_ref, None, None, za_ref, sg_ref, o_ref


def _key_value_chunks(kc_ref, vtc_ref, kl_ref, vtl_ref):
    n_ctx_chunks = vtc_ref.shape[2]
    n_lat_chunks = 0 if vtl_ref is None else vtl_ref.shape[2]

    def ctx(c):
        return kc_ref[0, c * KEY_CHUNK:(c + 1) * KEY_CHUNK, :], vtc_ref[0, 0, c]

    def lat(c):
        return kl_ref[0, pl.ds(pl.multiple_of(c * KEY_CHUNK, KEY_CHUNK), KEY_CHUNK), :], vtl_ref[0, 0, c]

    return n_ctx_chunks, n_lat_chunks, ctx, lat


def _query_maps(q):
    lane = lax.broadcasted_iota(jnp.int32, q.shape, 1)
    zero = jnp.zeros_like(q)
    return jnp.where(lane < HEAD_DIM, q, zero), jnp.where(lane >= HEAD_DIM, q, zero)


def _attn_finish(o_t, za, sg_ref, lam_init):
    o = o_t.T
    ms = jnp.mean(o * o, axis=-1, keepdims=True)
    y = o * lax.rsqrt(ms + EPS) * sg_ref[...] * (1.0 - lam_init)
    return (y * za.astype(F32)).astype(BF16)


def _attn_kernel(*refs, has_lat, lam_init):
    (scal_ref, q_ref, kc_ref, vtc_ref, kl_ref, vtl_ref, za_ref, sg_ref,
     o_ref) = _attn_refs(refs, has_lat)
    tq = q_ref.shape[1]
    n_ctx_chunks, n_lat_chunks, ctx_chunk, lat_chunk = _key_value_chunks(kc_ref, vtc_ref, kl_ref, vtl_ref)
    q_maps = _query_maps(q_ref[0])

    def step(kblk, vtb, carry):
        out = []
        for mp in range(2):
            m_old, l_old, o_old = carry[3 * mp:3 * mp + 3]
            s = lax.dot_general(kblk, q_maps[mp], (((1,), (1,)), ((), ())),
                                preferred_element_type=F32)
            m_new = jnp.maximum(m_old, jnp.max(s, axis=0, keepdims=True))
            p = jnp.exp(s - m_new)
            alpha = jnp.exp(m_old - m_new)
            l_new = alpha * l_old + jnp.sum(p, axis=0, keepdims=True)
            o_new = alpha * o_old + jnp.dot(vtb, p.astype(BF16), preferred_element_type=F32)
            out += [m_new, l_new, o_new]
        return tuple(out)

    m0 = jnp.full((1, tq), -1e30, F32)
    l0 = jnp.zeros((1, tq), F32)
    o0 = jnp.zeros((HEAD_LANES, tq), F32)
    carry = (m0, l0, o0, m0, l0, o0)
    for c in range(n_ctx_chunks):
        carry = step(*ctx_chunk(c), carry)
    if n_lat_chunks:
        carry = lax.fori_loop(0, n_lat_chunks, lambda c, acc: step(*lat_chunk(c), acc), carry)
    _, l_a, o_a, _, l_b, o_b = carry
    o_ref[0] = _attn_finish(o_a / l_a - scal_ref[0] * (o_b / l_b), za_ref[0], sg_ref, lam_init)


def _attn_bounded_kernel(*refs, has_lat, lam_init):
    (scal_ref, q_ref, kc_ref, vtc_ref, kl_ref, vtl_ref, za_ref, sg_ref,
     o_ref) = _attn_refs(refs, has_lat)
    tq = q_ref.shape[1]
    n_ctx_chunks, n_lat_chunks, ctx_chunk, lat_chunk = _key_value_chunks(kc_ref, vtc_ref, kl_ref, vtl_ref)
    n_chunks = n_ctx_chunks + n_lat_chunks

    def chunk(c):
        return ctx_chunk(c) if c < n_ctx_chunks else lat_chunk(c - n_ctx_chunks)

    shift = scal_ref[1]
    steps = [(t, c) for t in range(tq // Q_SUB) for c in range(n_chunks)]
    q_maps = {}

    def scores(t, c):
        if t not in q_maps:
            q_maps[t] = _query_maps(q_ref[0, t * Q_SUB:(t + 1) * Q_SUB, :])
        kblk = chunk(c)[0]
        return [lax.dot_general(kblk, q_maps[t][mp], (((1,), (1,)), ((), ())),
                                preferred_element_type=F32) for mp in range(2)]

    s_next = scores(*steps[0])
    for i, (t, c) in enumerate(steps):
        if c == 0:
            l_acc = [jnp.zeros((SUBLANES, Q_SUB), F32) for _ in range(2)]
            o_acc = [jnp.zeros((HEAD_LANES, Q_SUB), F32) for _ in range(2)]
        s_cur = s_next
        if i + 1 < len(steps):
            s_next = scores(*steps[i + 1])
        vtb = chunk(c)[1]
        for mp in range(2):
            p = jnp.exp(s_cur[mp] - shift)
            l_acc[mp] = l_acc[mp] + jnp.sum(p.reshape(KEY_CHUNK // SUBLANES, SUBLANES, Q_SUB), axis=0)
            o_acc[mp] = o_acc[mp] + jnp.dot(vtb, p.astype(BF16), preferred_element_type=F32)
        if c == n_chunks - 1:
            l_a = jnp.sum(l_acc[0], axis=0, keepdims=True)
            l_b = jnp.sum(l_acc[1], axis=0, keepdims=True)
            rows = slice(t * Q_SUB, (t + 1) * Q_SUB)
            o_ref[0, rows, :] = _attn_finish(o_acc[0] / l_a - scal_ref[0] * (o_acc[1] / l_b),
                                             za_ref[0, rows, :], sg_ref, lam_init)


def _attention(scal, proj_q, proj_ctx, vt_ctx, proj_lat, vt_lat, subln_g, *, tq, att_width, lam_init,
               bounded, name):
    bsz, lq, _ = proj_q.shape
    n_ctx = proj_ctx.shape[1]
    nh = att_width // HEAD_LANES
    k_off, z_off = nh, 3 * nh
    vt_spec = lambda vt: pl.BlockSpec((1, 1) + vt.shape[2:], lambda b, h, i: (b, h, 0, 0, 0))
    in_specs = [
        pl.BlockSpec(memory_space=pltpu.SMEM),
        pl.BlockSpec((1, tq, HEAD_LANES), lambda b, h, i: (b, i, h)),
        pl.BlockSpec((1, n_ctx, HEAD_LANES), lambda b, h, i: (b, 0, k_off + h)),
        vt_spec(vt_ctx),
    ]
    args = [scal, proj_q, proj_ctx, vt_ctx]
    if proj_lat is not None:
        n_lat = proj_lat.shape[1]
        in_specs += [pl.BlockSpec((1, n_lat, HEAD_LANES), lambda b, h, i: (b, 0, k_off + h)),
                     vt_spec(vt_lat)]
        args += [proj_lat, vt_lat]
    in_specs += [pl.BlockSpec((1, tq, HEAD_LANES), lambda b, h, i: (b, i, z_off + h)),
                 pl.BlockSpec((1, HEAD_LANES), lambda b, h, i: (0, 0))]
    args += [proj_q, subln_g.reshape(1, HEAD_LANES)]
    kern = functools.partial(_attn_bounded_kernel if bounded else _attn_kernel,
                             has_lat=proj_lat is not None, lam_init=lam_init)
    return pl.pallas_call(
        kern,
        grid=(bsz, nh, lq // tq),
        in_specs=in_specs,
        out_specs=pl.BlockSpec((1, tq, HEAD_LANES), lambda b, h, i: (b, i, h)),
        out_shape=jax.ShapeDtypeStruct((bsz, lq, att_width), BF16),
        compiler_params=_params("arbitrary", "arbitrary", "arbitrary"),
        name=name,
    )(*args)


def _conv_kernel(gv_ref, gg_ref, zb_ref, gvp_ref, ggp_ref, gvn_ref, ggn_ref,
                 w_ref, b_ref, lg_ref, lb_ref, o_ref, ypad_ref, ysh_ref, *, row_chunk):
    i = pl.program_id(1)
    n_tiles = pl.num_programs(1)
    tl = o_ref.shape[1]
    padded = tl + 2 * CONV_HALO

    def glu(val_ref, gate_ref, rows):
        return val_ref[0, rows, :].astype(F32) * jax.nn.sigmoid(gate_ref[0, rows, :].astype(F32))

    halo = slice(0, CONV_HALO)
    ypad_ref[0:CONV_HALO, :] = jnp.where(i > 0, glu(gvp_ref, ggp_ref, halo), 0.0)
    ypad_ref[CONV_HALO + tl:padded, :] = jnp.where(i < n_tiles - 1, glu(gvn_ref, ggn_ref, halo), 0.0)
    for r in range(tl // row_chunk):
        rows = slice(r * row_chunk, (r + 1) * row_chunk)
        ypad_ref[CONV_HALO + r * row_chunk:CONV_HALO + (r + 1) * row_chunk, :] = glu(gv_ref, gg_ref, rows)

    span = padded - SUBLANES
    for s in range(1, SUBLANES):
        for r in range(0, span, row_chunk):
            n = min(row_chunk, span - r)
            ysh_ref[s, r:r + n, :] = ypad_ref[r + s:r + s + n, :]

    first = CONV_HALO - CONV_K // 2
    for r in range(tl // row_chunk):
        acc = None
        for t in range(CONV_K):
            start = r * row_chunk + first + t
            phase, base = start % SUBLANES, start - start % SUBLANES
            src = ypad_ref[base:base + row_chunk, :] if phase == 0 else ysh_ref[phase, base:base + row_chunk, :]
            term = src.reshape(row_chunk // SUBLANES, SUBLANES, -1) * w_ref[t][None]
            acc = term if acc is None else acc + term
        acc = acc.reshape(row_chunk, -1) + b_ref[...]
        mu = jnp.mean(acc, axis=-1, keepdims=True)
        xc = acc - mu
        var = jnp.mean(xc * xc, axis=-1, keepdims=True)
        y = xc * lax.rsqrt(var + EPS) * lg_ref[...] + lb_ref[...]
        rows = slice(r * row_chunk, (r + 1) * row_chunk)
        o_ref[0, rows, :] = (_silu(y) * zb_ref[0, rows, :].astype(F32)).astype(BF16)


def _conv_branch(proj, conv_w, conv_b, cln_g, cln_b, *, tl, col0, width, name):
    bsz, length, _ = proj.shape
    n_tiles = length // tl
    halo_per_tile = tl // CONV_HALO
    last_halo = length // CONV_HALO - 1
    main = lambda off: pl.BlockSpec((1, tl, width), lambda b, i: (b, i, col0 + off))
    prev = lambda off: pl.BlockSpec(
        (1, CONV_HALO, width), lambda b, i: (b, jnp.maximum(i * halo_per_tile - 1, 0), col0 + off))
    nxt = lambda off: pl.BlockSpec(
        (1, CONV_HALO, width),
        lambda b, i: (b, jnp.minimum((i + 1) * halo_per_tile, last_halo), col0 + off))
    row = lambda k: pl.BlockSpec((k, width), lambda b, i: (0, 0))
    padded = tl + 2 * CONV_HALO
    return pl.pallas_call(
        functools.partial(_conv_kernel, row_chunk=32),
        grid=(bsz, n_tiles),
        in_specs=[main(0), main(1), main(2), prev(0), prev(1), nxt(0), nxt(1),
                  pl.BlockSpec((CONV_K, SUBLANES, width), lambda b, i: (0, 0, 0)),
                  row(1), row(1), row(1)],
        out_specs=pl.BlockSpec((1, tl, width), lambda b, i: (b, i, 0)),
        out_shape=jax.ShapeDtypeStruct((bsz, length, width), BF16),
        scratch_shapes=[pltpu.VMEM((padded, width), F32),
                        pltpu.VMEM((SUBLANES, padded, width), F32)],
        compiler_params=_params("arbitrary", "arbitrary"),
        name=name,
    )(proj, proj, proj, proj, proj, proj, proj,
      jnp.broadcast_to(conv_w[:, None, :], (CONV_K, SUBLANES, width)),
      conv_b.reshape(1, width), cln_g.reshape(1, width), cln_b.reshape(1, width))


def _out_ab_kernel(a_ref, c_ref, wa_ref, wc_ref, x_ref, gate_ref, o_ref):
    y = (jnp.dot(a_ref[...], wa_ref[...], preferred_element_type=F32)
         + jnp.dot(c_ref[...], wc_ref[...], preferred_element_type=F32))
    o_ref[...] = x_ref[...] + gate_ref[0] * y


def _out_ab(att, conv, w_bf, x2d, gate, rows_per_group, *, tm, tn, name):
    m, d = x2d.shape
    ka = att.shape[1]
    kc = conv.shape[1]
    tiles_per_group = rows_per_group // tm
    return pl.pallas_call(
        _out_ab_kernel,
        grid=(m // tm, d // tn),
        in_specs=[pl.BlockSpec((tm, ka), lambda i, j: (i, 0)),
                  pl.BlockSpec((tm, kc), lambda i, j: (i, 0)),
                  pl.BlockSpec((ka, tn), lambda i, j: (0, j)),
                  pl.BlockSpec((kc, tn), lambda i, j: (ka // kc, j)),
                  pl.BlockSpec((tm, tn), lambda i, j: (i, j)),
                  pl.BlockSpec((1, 1, tn), lambda i, j: (i // tiles_per_group, 0, j))],
        out_specs=pl.BlockSpec((tm, tn), lambda i, j: (i, j)),
        out_shape=jax.ShapeDtypeStruct((m, d), F32),
        compiler_params=_params("arbitrary", "arbitrary"),
        name=name,
    )(att, conv, w_bf, w_bf, x2d, gate)


def _out_c_kernel(of_ref, ob_ref, z_ref, g_ref, w_ref, x_ref, gate_ref, o_ref, y_ref, *, row_chunk):
    j = pl.program_id(1)
    tm, d = y_ref.shape

    @pl.when(j == 0)
    def _readout():
        def body(r, carry):
            rows = pl.ds(pl.multiple_of(r * row_chunk, row_chunk), row_chunk)
            for h in range(d // HEAD_LANES):
                cols = slice(h * HEAD_LANES, (h + 1) * HEAD_LANES)
                o = of_ref[rows, cols].astype(F32) + ob_ref[rows, cols].astype(F32)
                ms = jnp.mean(o * o, axis=-1, keepdims=True)
                y = o * lax.rsqrt(ms + EPS) * g_ref[...]
                y_ref[rows, cols] = (y * z_ref[rows, cols].astype(F32)).astype(BF16)
            return carry
        lax.fori_loop(0, tm // row_chunk, body, 0)

    y = jnp.dot(y_ref[...], w_ref[...], preferred_element_type=F32)
    o_ref[...] = x_ref[...] + gate_ref[0] * y


def _out_c(o_f, o_b, proj, z_col, onorm_g, w_bf, x2d, gate, rows_per_group, *, tm, tn, name):
    m, d = x2d.shape
    tiles_per_group = rows_per_group // tm
    return pl.pallas_call(
        functools.partial(_out_c_kernel, row_chunk=64),
        grid=(m // tm, d // tn),
        in_specs=[pl.BlockSpec((tm, d), lambda i, j: (i, 0)),
                  pl.BlockSpec((tm, d), lambda i, j: (i, 0)),
                  pl.BlockSpec((tm, d), lambda i, j: (i, z_col)),
                  pl.BlockSpec((1, HEAD_LANES), lambda i, j: (0, 0)),
                  pl.BlockSpec((d, tn), lambda i, j: (0, j)),
                  pl.BlockSpec((tm, tn), lambda i, j: (i, j)),
                  pl.BlockSpec((1, 1, tn), lambda i, j: (i // tiles_per_group, 0, j))],
        out_specs=pl.BlockSpec((tm, tn), lambda i, j: (i, j)),
        out_shape=jax.ShapeDtypeStruct((m, d), F32),
        scratch_shapes=[pltpu.VMEM((tm, d), BF16)],
        compiler_params=_params("arbitrary", "arbitrary"),
        name=name,
    )(o_f, o_b, proj, onorm_g.reshape(1, HEAD_LANES), w_bf, x2d, gate)


def _chunk_running_sum(x, tri_bf):
    hi = x.astype(BF16)
    lo = (x - hi.astype(F32)).astype(BF16)
    return (jnp.dot(tri_bf, hi, preferred_element_type=F32)
            + jnp.dot(tri_bf, lo, preferred_element_type=F32))


def _scan_kernel(*refs, has_s0, emit_o, emit_s):
    refs = list(refs)
    qf_ref, vf_ref, uf_ref, qb_ref, vb_ref, ub_ref, lb_ref, tri_ref = refs[:8]
    refs = refs[8:]
    s0_ref = refs.pop(0) if has_s0 else None
    of_ref, ob_ref = (refs.pop(0), refs.pop(0)) if emit_o else (None, None)
    sout_ref = refs.pop(0) if emit_s else None
    st_ref, = refs
    n = pl.program_id(2)
    tb = qf_ref.shape[1]

    @pl.when(n == 0)
    def _init():
        if has_s0:
            st_ref[...] = s0_ref[:, 0, 0]
        else:
            st_ref[...] = jnp.zeros_like(st_ref)

    sources = ((qf_ref, vf_ref, uf_ref, of_ref), (qb_ref, vb_ref, ub_ref, ob_ref))
    blk = tri_ref.shape[1]
    n_sub = tb // blk
    n_chunks = blk // HGRN_CHUNK
    work = []
    for d in range(2):
        for sb in (range(n_sub - 1, -1, -1) if d == 1 else range(n_sub)):
            work.append(dict(d=d, rows=slice(sb * blk, (sb + 1) * blk), base=sb * blk))

    for w in work:
        d = w["d"]
        lbd = lb_ref[d]
        sig = jax.nn.sigmoid(sources[d][2][0, w["rows"], :].astype(F32))
        logf = jnp.log(lbd + (1.0 - lbd) * sig)
        w["k"] = (1.0 - lbd) * (1.0 - sig)
        w["b"] = _chunk_running_sum(logf, tri_ref[d].astype(BF16))

    for w in work:
        d = w["d"]
        q_ref, v_ref = sources[d][:2]
        decay = jnp.exp(w["b"])
        k_inv = w["k"] / decay
        v_bf = v_ref[0, w["rows"], :]
        q_dec = (q_ref[0, w["rows"], :].astype(F32) * decay).astype(BF16)
        if emit_o:
            w["a"] = lax.dot_general(q_dec, k_inv.astype(BF16), (((1,), (1,)), ((), ())),
                                     preferred_element_type=F32)
        decay_t = decay.T
        chunks = []
        for c in (range(n_chunks - 1, -1, -1) if d == 1 else range(n_chunks)):
            rows = slice(c * HGRN_CHUNK, (c + 1) * HGRN_CHUNK)
            last = c * HGRN_CHUNK if d == 1 else (c + 1) * HGRN_CHUNK - 1
            k_end = (k_inv[rows] * decay[last:last + 1, :]).astype(BF16)
            kv = lax.dot_general(k_end, v_bf[rows], (((0,), (0,)), ((), ())),
                                 preferred_element_type=F32)
            chunks.append((rows, decay_t[:, last:last + 1], kv))
        w.update(q_dec=q_dec, v_bf=v_bf, chunks=chunks)

    if emit_o:
        for w in work:
            a = jnp.where(tri_ref[w["d"]] > 0.5, w["a"], 0.0).astype(BF16)
            w["o_intra"] = jnp.dot(a, w["v_bf"], preferred_element_type=F32)

    readouts = [[], []]
    for d in range(2):
        st = st_ref[d]
        for w in work:
            if w["d"] != d:
                continue
            for rows, decay_end, kv in w["chunks"]:
                readouts[d].append((w, rows, st.astype(BF16)))
                st = decay_end * st + kv
        st_ref[d] = st

    if emit_o:
        for pair in zip(*readouts):
            for d, (w, rows, st_bf) in enumerate(pair):
                o = w["o_intra"][rows] + jnp.dot(w["q_dec"][rows], st_bf, preferred_element_type=F32)
                sources[d][3][0, w["base"] + rows.start:w["base"] + rows.stop, :] = o.astype(BF16)

    if emit_s:
        @pl.when(n == pl.num_programs(2) - 1)
        def _final():
            sout_ref[:, 0, 0] = st_ref[...]


def _scan(proj, lb, s0, *, tb, n_heads, emit_o, emit_s, name):
    bsz, length, _ = proj.shape
    nb = length // tb
    hd = HEAD_LANES
    fwd = lambda off: pl.BlockSpec((1, tb, hd), lambda b, h, n: (b, n, off * n_heads + h))
    bwd = lambda off: pl.BlockSpec((1, tb, hd), lambda b, h, n: (b, nb - 1 - n, off * n_heads + h))
    state = pl.BlockSpec((2, 1, 1, hd, hd), lambda b, h, n: (0, b, h, 0, 0))
    blk = min(tb, SCAN_BLOCK)
    tok = jnp.arange(blk)
    same_chunk = (tok[:, None] // HGRN_CHUNK) == (tok[None, :] // HGRN_CHUNK)
    tri = jnp.stack([same_chunk & (tok[None, :] <= tok[:, None]),
                     same_chunk & (tok[None, :] >= tok[:, None])]).astype(F32)
    in_specs = [fwd(0), fwd(1), fwd(2), bwd(0), bwd(1), bwd(3),
                pl.BlockSpec((2, 1, hd), lambda b, h, n: (0, 0, h)),
                pl.BlockSpec((2, blk, blk), lambda b, h, n: (0, 0, 0))]
    args = [proj] * 6 + [lb, tri]
    if s0 is not None:
        in_specs.append(state)
        args.append(s0)
    out_specs, out_shape = [], []
    if emit_o:
        out_specs += [pl.BlockSpec((1, tb, hd), lambda b, h, n: (b, n, h)),
                      pl.BlockSpec((1, tb, hd), lambda b, h, n: (b, nb - 1 - n, h))]
        out_shape += [jax.ShapeDtypeStruct((bsz, length, n_heads * hd), BF16)] * 2
    if emit_s:
        out_specs.append(state)
        out_shape.append(jax.ShapeDtypeStruct((2, bsz, n_heads, hd, hd), F32))
    kern = functools.partial(_scan_kernel, has_s0=s0 is not None, emit_o=emit_o, emit_s=emit_s)
    return pl.pallas_call(
        kern,
        grid=(bsz, n_heads, nb),
        in_specs=in_specs,
        out_specs=out_specs,
        out_shape=out_shape,
        scratch_shapes=[pltpu.VMEM((2, hd, hd), F32)],
        compiler_params=_params("arbitrary", "arbitrary", "arbitrary"),
        name=name,
    )(*args)


def _rope_tables(length):
    half = HEAD_DIM // 2
    rows = length // GRID_W
    row = jnp.repeat(jnp.arange(rows), GRID_W).astype(F32)
    col = jnp.tile(jnp.arange(GRID_W), rows).astype(F32)
    inv = ROPE_THETA ** (-jnp.arange(0, half, 2, dtype=F32) / half)
    def axis_angles(pos):
        a = pos[:, None] * inv[None, :]
        return jnp.concatenate([a, a], axis=-1)
    ang = jnp.concatenate([axis_angles(row), axis_angles(col)], axis=-1)
    ang = jnp.concatenate([ang, ang], axis=-1)
    first_half = (jnp.arange(HEAD_LANES) % half) < half // 2
    sin = jnp.sin(ang)
    return jnp.cos(ang), jnp.where(first_half, -sin, 0.0), jnp.where(first_half, 0.0, sin)


def kernel(x, c, ctx, c_ctx, w_ada, b_ada, norm_g, w_in_ab, w_out_ab, qn_g, kn_g, lam_q1, lam_k1,
           lam_q2, lam_k2, subln_g, conv_w, conv_b, cln_g, cln_b, w_in_c, w_out_c, lb_gamma, onorm_g):
    bsz, seq, d = x.shape
    n_ctx = ctx.shape[1]
    depth = w_ada.shape[0]
    assert depth == 2 and w_in_ab.shape[0] == 1 and w_in_c.shape[0] == 1
    att_width = N_HEADS * HEAD_LANES
    conv_width = conv_w.shape[-1]
    assert w_in_ab.shape[-1] == 4 * att_width + 3 * conv_width and conv_width == att_width

    cond_rows = jnp.concatenate([c, c_ctx[None, :], jnp.zeros((8 - bsz - 1, d), F32)], axis=0)
    mod = _ada(cond_rows, w_ada, b_ada)
    def mods(l, rows):
        m = mod[l, rows][:, None, :]
        return m[..., :d], m[..., d:2 * d], m[..., 2 * d:]

    cos, sin_a, sin_b = _rope_tables(seq)
    gmat = jnp.kron(jnp.eye(2, dtype=F32), jnp.full((HEAD_DIM, HEAD_DIM), 1.0 / HEAD_DIM, F32)
                    ).astype(BF16)
    two = lambda g: jnp.concatenate([g, g]).reshape(1, HEAD_LANES)
    q_scale = HEAD_DIM ** -0.5

    x2d = x.reshape(bsz * seq, d)
    ctx2d = ctx.reshape(bsz * n_ctx, d)
    lat_rows = slice(0, bsz)
    ctx_rows = slice(bsz, bsz + 1)

    lam_init = 0.8 - 0.6 * math.exp(-0.3 * 0)
    lam = (jnp.exp(jnp.sum(lam_q1[0] * lam_k1[0])) - jnp.exp(jnp.sum(lam_q2[0] * lam_k2[0]))
           + lam_init).reshape(1)
    shift, scale, gate = mods(0, lat_rows)
    shift_c, scale_c, gate_c = mods(0, ctx_rows)
    w_in = w_in_ab[0].astype(BF16)
    w_out = w_out_ab[0].astype(BF16)
    ab_in = w_in.shape[1]
    qk_lat = (two(qn_g[0]), two(kn_g[0]), gmat, cos, sin_a, sin_b, att_width, q_scale)
    qk_ctx = (two(qn_g[0]), two(kn_g[0]), gmat, None, None, None, att_width, q_scale)
    gates = ((3 * att_width, 4 * att_width), (ab_in - conv_width, ab_in))
    p_lat, vt_lat = _proj(x2d, norm_g[0], scale, shift, w_in, seq, tm=1024, tn=1024, qk=qk_lat,
                          silu_cols=gates, seq_len=seq, name="proj_ab_lat")
    p_ctx, vt_ctx = _proj(ctx2d, norm_g[0], scale_c, shift_c, w_in, bsz * n_ctx, tm=1024, tn=1024,
                          qk=qk_ctx, silu_cols=gates, seq_len=n_ctx, name="proj_ab_ctx")
    p_lat = p_lat.reshape(bsz, seq, ab_in)
    p_ctx = p_ctx.reshape(bsz, n_ctx, ab_in)
    bound = math.sqrt(HEAD_DIM) * jnp.max(jnp.abs(qn_g[0])) * jnp.max(jnp.abs(kn_g[0]))
    scal = jnp.concatenate([lam, bound.reshape(1)])
    def attend(bounded):
        def run():
            lat = _attention(scal, p_lat, p_ctx, vt_ctx, p_lat, vt_lat, subln_g[0], tq=2048,
                             att_width=att_width, lam_init=lam_init, bounded=bounded,
                             name="attn_lat_bounded" if bounded else "attn_lat")
            ctx_ = _attention(scal, p_ctx, p_ctx, vt_ctx, None, None, subln_g[0], tq=n_ctx,
                              att_width=att_width, lam_init=lam_init, bounded=bounded,
                              name="attn_ctx_bounded" if bounded else "attn_ctx")
            return lat, ctx_
        return run
    att_lat, att_ctx = lax.cond(bound <= MAX_FIXED_SHIFT, attend(True), attend(False))
    conv_col0 = 4 * att_width // conv_width
    conv_lat = _conv_branch(p_lat, conv_w[0], conv_b[0], cln_g[0], cln_b[0], tl=256,
                            col0=conv_col0, width=conv_width, name="conv_lat")
    conv_ctx = _conv_branch(p_ctx, conv_w[0], conv_b[0], cln_g[0], cln_b[0], tl=256,
                            col0=conv_col0, width=conv_width, name="conv_ctx")
    x2d = _out_ab(att_lat.reshape(bsz * seq, att_width), conv_lat.reshape(bsz * seq, conv_width),
                  w_out, x2d, gate, seq, tm=512, tn=2048, name="out_ab_lat")
    ctx2d = _out_ab(att_ctx.reshape(bsz * n_ctx, att_width),
                    conv_ctx.reshape(bsz * n_ctx, conv_width),
                    w_out, ctx2d, gate_c, bsz * n_ctx, tm=512, tn=1024, name="out_ab_ctx")

    p = jax.nn.softmax(lb_gamma.astype(F32), axis=1)
    lb_all = jnp.cumsum(p, axis=1) - p[:, :1]
    lb = lb_all[:, 1].reshape(2, 1, d)
    shift, scale, gate = mods(1, lat_rows)
    shift_c, scale_c, _ = mods(1, ctx_rows)
    w_in = w_in_c[0].astype(BF16)
    w_out = w_out_c[0].astype(BF16)
    c_in = w_in.shape[1]
    gates = ((0, d), (4 * d, 5 * d))
    p_lat = _proj(x2d, norm_g[1], scale, shift, w_in, seq, tm=1024, tn=1024, silu_cols=gates,
                  name="proj_c_lat").reshape(bsz, seq, c_in)
    p_ctx = _proj(ctx2d, norm_g[1], scale_c, shift_c, w_in, bsz * n_ctx, tm=1024, tn=1024,
                  silu_cols=gates, name="proj_c_ctx").reshape(bsz, n_ctx, c_in)
    s_ctx, = _scan(p_ctx, lb, None, tb=n_ctx, n_heads=HGRN_HEADS, emit_o=False, emit_s=True,
                   name="scan_ctx")
    o_f, o_b = _scan(p_lat, lb, s_ctx, tb=2048, n_heads=HGRN_HEADS, emit_o=True, emit_s=False,
                     name="scan_lat")
    out = _out_c(o_f.reshape(bsz * seq, d), o_b.reshape(bsz * seq, d),
                 p_lat.reshape(bsz * seq, c_in), 4, onorm_g[0], w_out, x2d, gate, seq,
                 tm=512, tn=2048, name="out_c_lat")
    return out.reshape(bsz, seq, d)
```
